```python
import math
import jax, jax.numpy as jnp
from jax import lax
import numpy as np

D_MODEL = 1024
BATCH = 4
SEQ = 4096
DEPTH = 4
DEC_BATCH = 32
DEC_SEQ = 1
PAST_LEN = 8192
PAGE_SIZE = 128

N_MIXERS = 4
LN_EPS = 1e-5
RMS_EPS = 1e-6
DEEPNORM_ALPHA = (2.0 * DEPTH) ** 0.25
DEEPNORM_BETA = (8.0 * DEPTH) ** -0.25
D_FF = 2816
CONV_W = 3
D_CONV = D_MODEL
DA_HEADS = 8
DA_HD = D_MODEL // (2 * DA_HEADS)
ROT_DIM = DA_HD // 4
ROPE_THETA = 500000.0
Q_BLOCK = 128
DA_LAYER = 1
DA_LAMBDA_INIT = 0.8 - 0.6 * math.exp(-0.3 * DA_LAYER)
NEG_INF = -1e30
HG_HEADS = 8
HG_DK = D_MODEL // HG_HEADS
HG_DV = D_MODEL // HG_HEADS
GLA_HEADS = 4
GLA_KEY = D_MODEL // 2
GLA_DK = GLA_KEY // GLA_HEADS
GLA_DV = D_MODEL // GLA_HEADS
GLA_LOWRANK = 16
GLA_GATE_NORM = 16.0
CHUNK = 32

kernel_name = 'hybrid_conv_diffattn_hgrn2_gla_macaron_step'


def layer_norm(x, g, b):
    xf = x.astype(jnp.float32)
    mu = jnp.mean(xf, axis=-1, keepdims=True)
    var = jnp.mean(jnp.square(xf - mu), axis=-1, keepdims=True)
    y = (xf - mu) * lax.rsqrt(var + LN_EPS) * g.astype(jnp.float32) + b.astype(jnp.float32)
    return y.astype(x.dtype)


def rms_norm(x, g):
    xf = x.astype(jnp.float32)
    y = xf * lax.rsqrt(jnp.mean(xf * xf, axis=-1, keepdims=True) + RMS_EPS) * g.astype(jnp.float32)
    return y.astype(x.dtype)


def post_norm(x, delta, g, b):
    return layer_norm(DEEPNORM_ALPHA * x + delta, g, b)


def swiglu(x, w_up, w_down):
    a, u = jnp.split(x @ w_up, 2, axis=-1)
    return (jax.nn.silu(a) * u) @ w_down


def half_ffn_block(x, w_up, w_down, g, b):
    return post_norm(x, 0.5 * swiglu(x, w_up, w_down), g, b)


def short_conv_mixer(x, buf, w_in, conv_w, w_out):
    L = x.shape[1]
    b_gate, c_gate, h = jnp.split(x @ w_in, 3, axis=-1)
    u = c_gate * h
    u_ext = jnp.concatenate([buf.astype(u.dtype), u], axis=1)
    conv = sum(conv_w[j] * u_ext[:, j:j + L] for j in range(CONV_W))
    y = (b_gate * conv) @ w_out
    return y, u_ext[:, L:]


def partial_rope(x, pos):
    inv_freq = 1.0 / (ROPE_THETA ** (jnp.arange(0, ROT_DIM, 2, dtype=jnp.float32) / ROT_DIM))
    ang = pos.astype(jnp.float32)[:, None] * inv_freq[None, :]
    cos = jnp.cos(ang)[None, :, None, :]
    sin = jnp.sin(ang)[None, :, None, :]
    xr = x[..., :ROT_DIM].astype(jnp.float32)
    x1, x2 = xr[..., :ROT_DIM // 2], xr[..., ROT_DIM // 2:]
    rot = jnp.concatenate([x1 * cos - x2 * sin, x2 * cos + x1 * sin], axis=-1)
    return jnp.concatenate([rot.astype(x.dtype), x[..., ROT_DIM:]], axis=-1)


def diff_lambda(lq1, lk1, lq2, lk2):
    f32 = jnp.float32
    return (jnp.exp(jnp.sum(lq1.astype(f32) * lk1.astype(f32)))
            - jnp.exp(jnp.sum(lq2.astype(f32) * lk2.astype(f32))) + DA_LAMBDA_INIT)


def diff_qkv(x, pos, w_in):
    n, L, _ = x.shape
    q, k, v = jnp.split(x @ w_in, 3, axis=-1)
    q = partial_rope(q.reshape(n, L, 2 * DA_HEADS, DA_HD), pos)
    k = partial_rope(k.reshape(n, L, 2 * DA_HEADS, DA_HD), pos)
    v = v.reshape(n, L, DA_HEADS, 2 * DA_HD)
    return q, k, v


def diff_attend(q, k, v, q_pos, k_pos, lam):
    n, lq = q.shape[0], q.shape[1]
    s = jnp.einsum('nqcd,nkcd->ncqk', q, k, preferred_element_type=jnp.float32) * (DA_HD ** -0.5)
    mask = k_pos[None, :] <= q_pos[:, None]
    p = jax.nn.softmax(jnp.where(mask[None, None], s, NEG_INF), axis=-1)
    p = p.reshape(n, DA_HEADS, 2, lq, -1)
    a = p[:, :, 0] - lam * p[:, :, 1]
    return jnp.einsum('nhqk,nkhe->nqhe', a.astype(v.dtype), v)


def diff_output(o, subln_g, w_out):
    n, L = o.shape[0], o.shape[1]
    o = rms_norm(o, subln_g) * (1.0 - DA_LAMBDA_INIT)
    return o.reshape(n, L, D_MODEL) @ w_out


def diff_attn_prompt(x, lam, w_in, subln_g, w_out):
    n, L, _ = x.shape
    pos = jnp.arange(L, dtype=jnp.int32)
    q, k, v = diff_qkv(x, pos, w_in)
    n_blk = L // Q_BLOCK
    q_blocks = q.reshape(n, n_blk, Q_BLOCK, 2 * DA_HEADS, DA_HD).transpose(1, 0, 2, 3, 4)
    pos_blocks = pos.reshape(n_blk, Q_BLOCK)
    o = lax.map(lambda blk: diff_attend(blk[0], k, v, blk[1], pos, lam), (q_blocks, pos_blocks))
    o = o.transpose(1, 0, 2, 3, 4).reshape(n, L, DA_HEADS, 2 * DA_HD)
    return diff_output(o, subln_g, w_out), k, v


def diff_attn_sample(x, cache_k, cache_v, page_table, lam, w_in, subln_g, w_out):
    n, L, _ = x.shape
    past = page_table.shape[1] * PAGE_SIZE
    pos = past + jnp.arange(L, dtype=jnp.int32)
    q, k, v = diff_qkv(x, pos, w_in)
    k_past = cache_k[page_table].reshape(n, past, 2 * DA_HEADS, DA_HD).astype(k.dtype)
    v_past = cache_v[page_table].reshape(n, past, DA_HEADS, 2 * DA_HD).astype(v.dtype)
    k_all = jnp.concatenate([k_past, k], axis=1)
    v_all = jnp.concatenate([v_past, v], axis=1)
    k_pos = jnp.arange(past + L, dtype=jnp.int32)
    o = diff_attend(q, k_all, v_all, pos, k_pos, lam)
    return diff_output(o, subln_g, w_out), k, v


def chunked_gated_linear(q, k, v, log_f, s0):
    n, L, H, _ = q.shape
    dv = v.shape[-1]
    pad = (-L) % CHUNK
    nc = (L + pad) // CHUNK

    def blocks(a):
        a = jnp.pad(a.astype(jnp.float32), ((0, 0), (0, pad), (0, 0), (0, 0)))
        return a.reshape(n, nc, CHUNK, H, a.shape[-1]).transpose(1, 0, 3, 2, 4)

    causal = jnp.tril(jnp.ones((CHUNK, CHUNK), dtype=bool))

    def step(S, blk):
        qb, kb, vb, gb = blk
        G = jnp.cumsum(gb, axis=2)
        g_last = G[:, :, -1:, :]
        q_dec = qb * jnp.exp(G)
        k_dec = kb * jnp.exp(-G)
        scores = jnp.where(causal, jnp.einsum('nhcd,nhsd->nhcs', q_dec, k_dec), 0.0)
        o = jnp.einsum('nhcd,nhde->nhce', q_dec, S) + jnp.einsum('nhcs,nhse->nhce', scores, vb)
        k_tail = kb * jnp.exp(g_last - G)
        S = S * jnp.exp(g_last[:, :, 0, :])[..., None] + jnp.einsum('nhcd,nhce->nhde', k_tail, vb)
        return S, o

    s_fin, o = lax.scan(step, s0.astype(jnp.float32), (blocks(q), blocks(k), blocks(v), blocks(log_f)))
    o = o.transpose(1, 0, 3, 2, 4).reshape(n, nc * CHUNK, H, dv)[:, :L]
    return o.astype(v.dtype), s_fin.astype(s0.dtype)


def hgrn2_lower_bounds(lb_logits):
    p = jax.nn.softmax(lb_logits.astype(jnp.float32), axis=0)
    return jnp.cumsum(p, axis=0) - p[0:1]


def hgrn2_mixer(x, s0, lb, w_in, norm_g, w_out):
    n, L, _ = x.shape
    q, f, i_in = jnp.split(x @ w_in, 3, axis=-1)
    forget = lb + (1.0 - lb) * jax.nn.sigmoid(f.astype(jnp.float32))
    heads = lambda a: a.reshape(n, L, HG_HEADS, -1)
    o, s = chunked_gated_linear(heads(jax.nn.silu(q)), heads(1.0 - forget), heads(i_in),
                                heads(jnp.log(forget)), s0)
    o = rms_norm(o, norm_g).reshape(n, L, D_MODEL)
    return o @ w_out, s


def gla_mixer(x, s0, w_in, w_gk2, b_gk2, norm_g, w_out):
    n, L, _ = x.shape
    q, k, v, g, gk = jnp.split(
        x @ w_in, [GLA_KEY, 2 * GLA_KEY, 2 * GLA_KEY + D_MODEL, 2 * GLA_KEY + 2 * D_MODEL], axis=-1)
    log_f = jax.nn.log_sigmoid((gk @ w_gk2 + b_gk2).astype(jnp.float32)) / GLA_GATE_NORM
    heads = lambda a: a.reshape(n, L, GLA_HEADS, -1)
    o, s = chunked_gated_linear(heads(q * (GLA_DK ** -0.5)), heads(k), heads(v), heads(log_f), s0)
    o = rms_norm(o, norm_g) * jax.nn.silu(heads(g))
    return o.reshape(n, L, D_MODEL) @ w_out, s


def setup_inputs(seed: int = 0) -> dict:
    key = jax.random.key(seed)
    ks = list(jax.random.split(key, 40))
    cnt = [0]

    def nxt():
        cnt[0] += 1
        return ks[cnt[0] - 1]

    def nrm(shape, scale=1.0):
        return jax.random.normal(nxt(), shape, jnp.float32) * scale

    d = D_MODEL
    n_pages = PAST_LEN // PAGE_SIZE
    n_used = DEC_BATCH * n_pages
    n_pool = n_used + max(1, n_used // 4)
    page_table = jax.random.permutation(nxt(), n_pool)[:n_used].reshape(DEC_BATCH, n_pages).astype(jnp.int32)
    gla_in_width = 2 * GLA_KEY + 2 * d + GLA_LOWRANK
    return {
        'x_prompt': nrm((BATCH, SEQ, d)),
        'x_sample': nrm((DEC_BATCH, DEC_SEQ, d)),
        'state_conv': nrm((DEC_BATCH, CONV_W - 1, D_CONV)),
        'cache_k': nrm((n_pool, PAGE_SIZE, 2 * DA_HEADS, DA_HD)),
        'cache_v': nrm((n_pool, PAGE_SIZE, DA_HEADS, 2 * DA_HD)),
        'page_table': page_table,
        'state_hgrn': nrm((DEC_BATCH, HG_HEADS, HG_DK, HG_DV), 0.5),
        'state_gla': nrm((DEC_BATCH, GLA_HEADS, GLA_DK, GLA_DV)),
        'ffn_w_up': nrm((DEPTH, 2, d, 2 * D_FF), d ** -0.5),
        'ffn_w_down': nrm((DEPTH, 2, D_FF, d), D_FF ** -0.5 * DEEPNORM_BETA),
        'ln_g': 1.0 + nrm((DEPTH, 3, d), 0.02),
        'ln_b': nrm((DEPTH, 3, d), 0.02),
        'conv_w_in': nrm((d, 3 * D_CONV), d ** -0.5),
        'conv_w': nrm((CONV_W, D_CONV), CONV_W ** -0.5),
        'conv_w_out': nrm((D_CONV, d), D_CONV ** -0.5 * DEEPNORM_BETA),
        'da_w_in': nrm((d, 3 * d), d ** -0.5),
        'da_lambda_q1': nrm((DA_HD,), 0.1),
        'da_lambda_k1': nrm((DA_HD,), 0.1),
        'da_lambda_q2': nrm((DA_HD,), 0.1),
        'da_lambda_k2': nrm((DA_HD,), 0.1),
        'da_subln_g': 1.0 + nrm((2 * DA_HD,), 0.02),
        'da_w_out': nrm((d, d), d ** -0.5 * DEEPNORM_BETA),
        'hg_w_in': nrm((d, 3 * d), d ** -0.5),
        'hg_lb_logits': nrm((DEPTH, d), 0.1),
        'hg_norm_g': 1.0 + nrm((HG_DV,), 0.02),
        'hg_w_out': nrm((d, d), d ** -0.5 * DEEPNORM_BETA),
        'gla_w_in': nrm((d, gla_in_width), d ** -0.5),
        'gla_w_gk2': nrm((GLA_LOWRANK, GLA_KEY), GLA_LOWRANK ** -0.5),
        'gla_b_gk2': nrm((GLA_KEY,), 0.1),
        'gla_norm_g': 1.0 + nrm((GLA_DV,), 0.02),
        'gla_w_out': nrm((d, d), d ** -0.5 * DEEPNORM_BETA),
    }


def reference(x_prompt, x_sample, state_conv, cache_k, cache_v, page_table, state_hgrn, state_gla,
              ffn_w_up, ffn_w_down, ln_g, ln_b,
              conv_w_in, conv_w, conv_w_out,
              da_w_in, da_lambda_q1, da_lambda_k1, da_lambda_q2, da_lambda_k2, da_subln_g, da_w_out,
              hg_w_in, hg_lb_logits, hg_norm_g, hg_w_out,
              gla_w_in, gla_w_gk2, gla_b_gk2, gla_norm_g, gla_w_out):
    xp, xs = x_prompt, x_sample
    n_p = xp.shape[0]
    for i in range(DEPTH):
        kind = i % N_MIXERS
        xp = half_ffn_block(xp, ffn_w_up[i, 0], ffn_w_down[i, 0], ln_g[i, 0], ln_b[i, 0])
        xs = half_ffn_block(xs, ffn_w_up[i, 0], ffn_w_down[i, 0], ln_g[i, 0], ln_b[i, 0])
        if kind == 0:
            buf0 = jnp.zeros((n_p, CONV_W - 1, D_CONV), xp.dtype)
            mp, conv_p = short_conv_mixer(xp, buf0, conv_w_in, conv_w, conv_w_out)
            ms, conv_s = short_conv_mixer(xs, state_conv, conv_w_in, conv_w, conv_w_out)
        elif kind == 1:
            lam = diff_lambda(da_lambda_q1, da_lambda_k1, da_lambda_q2, da_lambda_k2)
            mp, k_p, v_p = diff_attn_prompt(xp, lam, da_w_in, da_subln_g, da_w_out)
            ms, k_s, v_s = diff_attn_sample(xs, cache_k, cache_v, page_table, lam,
                                            da_w_in, da_subln_g, da_w_out)
        elif kind == 2:
            lb = hgrn2_lower_bounds(hg_lb_logits)[i]
            s0 = jnp.zeros((n_p, HG_HEADS, HG_DK, HG_DV), xp.dtype)
            mp, hg_p = hgrn2_mixer(xp, s0, lb, hg_w_in, hg_norm_g, hg_w_out)
            ms, hg_s = hgrn2_mixer(xs, state_hgrn, lb, hg_w_in, hg_norm_g, hg_w_out)
        else:
            s0 = jnp.zeros((n_p, GLA_HEADS, GLA_DK, GLA_DV), xp.dtype)
            mp, gla_p = gla_mixer(xp, s0, gla_w_in, gla_w_gk2, gla_b_gk2, gla_norm_g, gla_w_out)
            ms, gla_s = gla_mixer(xs, state_gla, gla_w_in, gla_w_gk2, gla_b_gk2, gla_norm_g, gla_w_out)
        xp = post_norm(xp, mp, ln_g[i, 1], ln_b[i, 1])
        xs = post_norm(xs, ms, ln_g[i, 1], ln_b[i, 1])
        xp = half_ffn_block(xp, ffn_w_up[i, 1], ffn_w_down[i, 1], ln_g[i, 2], ln_b[i, 2])
        xs = half_ffn_block(xs, ffn_w_up[i, 1], ffn_w_down[i, 1], ln_g[i, 2], ln_b[i, 2])
    return (xp, xs, conv_p, conv_s, k_p, v_p, k_s, v_s, hg_p, hg_s, gla_p, gla_s)
```

```python
import functools
import math

import jax
import jax.numpy as jnp
from jax import lax
from jax.experimental import pallas as pl
from jax.experimental.pallas import tpu as pltpu

F32 = jnp.float32
BF16 = jnp.bfloat16

DEPTH = 4
LN_EPS = 1e-5
RMS_EPS = 1e-6
ALPHA = (2.0 * DEPTH) ** 0.25
CONV_W = 3
DA_HEADS = 8
DA_HD = 64
ROT_DIM = DA_HD // 4
ROPE_THETA = 500000.0
DA_LAMBDA_INIT = 0.8 - 0.6 * math.exp(-0.3 * 1)
NEG_INF = -1e30
PAGE_SIZE = 128
HG_HEADS = 8
GLA_HEADS = 4
GLA_LOWRANK = 16
GLA_GATE_NORM = 16.0

LANES = 128
SUBLANES = 8
VMEM_LIMIT = 56 * 1024 * 1024

TOKEN_TILE = 512
FFN_CHUNK = 1408
MIX_TILE = 256
REC_CHUNK = 64
ATT_TILE = 512


def _mm(a, b):
    return jnp.dot(a, b, preferred_element_type=F32)


def _mm_nt(a, b):
    return lax.dot_general(a, b, (((1,), (1,)), ((), ())), preferred_element_type=F32)


def _mm_tn(a, b):
    return lax.dot_general(a, b, (((0,), (0,)), ((), ())), preferred_element_type=F32)


def _layer_norm(z, g, b):
    mu = jnp.mean(z, axis=-1, keepdims=True)
    zc = z - mu
    var = jnp.mean(zc * zc, axis=-1, keepdims=True)
    return zc * lax.rsqrt(var + LN_EPS) * g + b


def _rms_norm(o, g):
    return o * lax.rsqrt(jnp.mean(o * o, axis=-1, keepdims=True) + RMS_EPS) * g


def _silu(a):
    return a * jax.nn.sigmoid(a)


def _params(n_axes):
    return pltpu.CompilerParams(dimension_semantics=("arbitrary",) * n_axes,
                                vmem_limit_bytes=VMEM_LIMIT)


def _const_spec(shape):
    nd = len(shape)
    return pl.BlockSpec(shape, lambda *_: (0,) * nd)


def _ffn_body(x_ref, wa_ref, wu_ref, wd_ref, g_ref, b_ref, o_ref, xb_ref):
    j = pl.program_id(1)

    @pl.when(j == 0)
    def _init():
        xb_ref[...] = x_ref[...].astype(BF16)
        o_ref[...] = jnp.zeros_like(o_ref)

    xb = xb_ref[...]
    a = _mm(xb, wa_ref[...])
    u = _mm(xb, wu_ref[...])
    h = (_silu(a) * u).astype(BF16)
    o_ref[...] += _mm(h, wd_ref[...])

    @pl.when(j == pl.num_programs(1) - 1)
    def _finish():
        z = ALPHA * x_ref[...] + 0.5 * o_ref[...]
        o_ref[...] = _layer_norm(z, g_ref[...], b_ref[...])


def _ffn(x, w_up, w_down, g, b):
    t, d = x.shape
    f = w_down.shape[0]
    tm = min(TOKEN_TILE, t)
    fc = FFN_CHUNK if f % FFN_CHUNK == 0 else f
    nf = f // fc
    return pl.pallas_call(
        _ffn_body,
        grid=(t // tm, nf),
        in_specs=[
            pl.BlockSpec((tm, d), lambda i, j: (i, 0)),
            pl.BlockSpec((d, fc), lambda i, j: (0, j)),
            pl.BlockSpec((d, fc), lambda i, j: (0, j + nf)),
            pl.BlockSpec((fc, d), lambda i, j: (j, 0)),
            _const_spec((1, d)),
            _const_spec((1, d)),
        ],
        out_specs=pl.BlockSpec((tm, d), lambda i, j: (i, 0)),
        out_shape=jax.ShapeDtypeStruct((t, d), F32),
        scratch_shapes=[pltpu.VMEM((tm, d), BF16)],
        compiler_params=_params(2),
        name="ffn",
    )(x, w_up, w_up, w_down, g, b)


def _proj_norm_body(y_ref, x_ref, w_ref, g_ref, b_ref, o_ref):
    y = _mm(y_ref[...], w_ref[...])
    o_ref[...] = _layer_norm(ALPHA * x_ref[...] + y, g_ref[...], b_ref[...])


def _proj_norm(y, x, w, g, b):
    t, d = x.shape
    tm = min(TOKEN_TILE, t)
    return pl.pallas_call(
        _proj_norm_body,
        grid=(t // tm,),
        in_specs=[
            pl.BlockSpec((tm, d), lambda i: (i, 0)),
            pl.BlockSpec((tm, d), lambda i: (i, 0)),
            _const_spec((d, d)),
            _const_spec((1, d)),
            _const_spec((1, d)),
        ],
        out_specs=pl.BlockSpec((tm, d), lambda i: (i, 0)),
        out_shape=jax.ShapeDtypeStruct((t, d), F32),
        compiler_params=_params(1),
        name="proj_norm",
    )(y, x, w, g, b)


def _conv_prompt_body(x_ref, win_ref, cw_ref, wout_ref, g_ref, b_ref, o_ref, st_ref, ubuf_ref):
    t = pl.program_id(1)
    tm, d = x_ref.shape

    @pl.when(t == 0)
    def _reset():
        ubuf_ref[0:SUBLANES, :] = jnp.zeros((SUBLANES, d), F32)

    x = x_ref[...]
    p = _mm(x.astype(BF16), win_ref[...])
    u = p[:, d:2 * d] * p[:, 2 * d:]
    ubuf_ref[SUBLANES:, :] = u
    ue = ubuf_ref[...]
    um1 = pltpu.roll(ue, 1, 0)[SUBLANES:, :]
    um2 = pltpu.roll(ue, 2, 0)[SUBLANES:, :]
    cw = cw_ref[...]
    conv = cw[0:1, :] * um2 + cw[1:2, :] * um1 + cw[2:3, :] * u
    y = _mm((p[:, :d] * conv).astype(BF16), wout_ref[...])
    o_ref[...] = _layer_norm(ALPHA * x + y, g_ref[...], b_ref[...])
    ubuf_ref[0:SUBLANES, :] = ubuf_ref[tm:tm + SUBLANES, :]

    @pl.when(t == pl.num_programs(1) - 1)
    def _state():
        st_ref[0] = ubuf_ref[tm + SUBLANES - (CONV_W - 1):tm + SUBLANES, :]


def _conv_prompt(x, n_seq, w_in, conv_w, w_out, g, b):
    t, d = x.shape
    seq = t // n_seq
    tm = min(TOKEN_TILE, seq)
    nt = seq // tm
    return pl.pallas_call(
        _conv_prompt_body,
        grid=(n_seq, nt),
        in_specs=[
            pl.BlockSpec((tm, d), lambda s, i: (s * nt + i, 0)),
            _const_spec((d, 3 * d)),
            _const_spec((CONV_W, d)),
            _const_spec((d, d)),
            _const_spec((1, d)),
            _const_spec((1, d)),
        ],
        out_specs=[
            pl.BlockSpec((tm, d), lambda s, i: (s * nt + i, 0)),
            pl.BlockSpec((1, CONV_W - 1, d), lambda s, i: (s, 0, 0)),
        ],
        out_shape=[
            jax.ShapeDtypeStruct((t, d), F32),
            jax.ShapeDtypeStruct((n_seq, CONV_W - 1, d), F32),
        ],
        scratch_shapes=[pltpu.VMEM((tm + SUBLANES, d), F32)],
        compiler_params=_params(2),
        name="conv_prompt",
    )(x, w_in, conv_w, w_out, g, b)


def _conv_sample_body(x_ref, s0_ref, s1_ref, win_ref, cw_ref, wout_ref, g_ref, b_ref,
                      o_ref, u_ref):
    d = x_ref.shape[1]
    x = x_ref[...]
    p = _mm(x.astype(BF16), win_ref[...])
    u = p[:, d:2 * d] * p[:, 2 * d:]
    cw = cw_ref[...]
    conv = cw[0:1, :] * s0_ref[...] + cw[1:2, :] * s1_ref[...] + cw[2:3, :] * u
    y = _mm((p[:, :d] * conv).astype(BF16), wout_ref[...])
    o_ref[...] = _layer_norm(ALPHA * x + y, g_ref[...], b_ref[...])
    u_ref[...] = u


def _conv_sample(x, s0, s1, w_in, conv_w, w_out, g, b):
    n, d = x.shape
    return pl.pallas_call(
        _conv_sample_body,
        out_shape=[jax.ShapeDtypeStruct((n, d), F32), jax.ShapeDtypeStruct((n, d), F32)],
        compiler_params=pltpu.CompilerParams(vmem_limit_bytes=VMEM_LIMIT),
        name="conv_sample",
    )(x, s0, s1, w_in, conv_w, w_out, g, b)


def _rope_tables(pos):
    inv_freq = 1.0 / (ROPE_THETA ** (jnp.arange(0, ROT_DIM, 2, dtype=F32) / ROT_DIM))
    ang = pos.astype(F32)[:, None] * inv_freq[None, :]
    cos, sin = jnp.cos(ang), jnp.sin(ang)
    half = ROT_DIM // 2
    n = pos.shape[0]
    rest = DA_HD - ROT_DIM
    c64 = jnp.concatenate([cos, cos, jnp.ones((n, rest), F32)], axis=1)
    up64 = jnp.concatenate([-sin, jnp.zeros((n, DA_HD - half), F32)], axis=1)
    dn64 = jnp.concatenate([jnp.zeros((n, half), F32), sin, jnp.zeros((n, rest), F32)], axis=1)
    rep = LANES // DA_HD
    return (jnp.tile(c64, (1, rep)), jnp.tile(up64, (1, rep)), jnp.tile(dn64, (1, rep)),
            cos.T, sin.T)


def _rope_rows(x, c, s_up, s_dn):
    half = ROT_DIM // 2
    blocks = []
    for i in range(x.shape[1] // LANES):
        xb = x[:, i * LANES:(i + 1) * LANES]
        blocks.append(xb * c + pltpu.roll(xb, LANES - half, 1) * s_up + pltpu.roll(xb, half, 1) * s_dn)
    return jnp.concatenate(blocks, axis=1)


def _da_qkv_prompt_body(x_ref, wq_ref, wkt_ref, wv_ref, c_ref, up_ref, dn_ref, ct_ref, st_ref,
                        q_ref, kt_ref, ktb_ref, v_ref, vb_ref):
    tm, d = x_ref.shape
    half = ROT_DIM // 2
    xb = x_ref[...].astype(BF16)
    q = _rope_rows(_mm(xb, wq_ref[...]), c_ref[...], up_ref[...], dn_ref[...])
    q_ref[...] = (q * (DA_HD ** -0.5)).astype(BF16)
    v = _mm(xb, wv_ref[...])
    v_ref[...] = v
    vb_ref[...] = v.astype(BF16)
    kt = _mm_nt(wkt_ref[...], xb).reshape(2 * DA_HEADS, DA_HD, tm)
    ct = ct_ref[...]
    st = st_ref[...]
    x1 = kt[:, 0:half, :]
    x2 = kt[:, half:ROT_DIM, :]
    kt = jnp.concatenate([x1 * ct - x2 * st, x2 * ct + x1 * st, kt[:, ROT_DIM:, :]], axis=1)
    kt = kt.reshape(d, tm)
    kt_ref[0] = kt
    ktb_ref[0] = kt.astype(BF16)


def _da_qkv_prompt(x, n_seq, wq, wkt, wv, tables):
    t, d = x.shape
    seq = t // n_seq
    tm = min(TOKEN_TILE, seq)
    nt = seq // tm
    c, up, dn, ct, st = tables
    half = ROT_DIM // 2
    row = lambda s, i: (s * nt + i, 0)
    return pl.pallas_call(
        _da_qkv_prompt_body,
        grid=(n_seq, nt),
        in_specs=[
            pl.BlockSpec((tm, d), row),
            _const_spec((d, d)),
            _const_spec((d, d)),
            _const_spec((d, d)),
            pl.BlockSpec((tm, LANES), lambda s, i: (i, 0)),
            pl.BlockSpec((tm, LANES), lambda s, i: (i, 0)),
            pl.BlockSpec((tm, LANES), lambda s, i: (i, 0)),
            pl.BlockSpec((half, tm), lambda s, i: (0, i)),
            pl.BlockSpec((half, tm), lambda s, i: (0, i)),
        ],
        out_specs=[
            pl.BlockSpec((tm, d), row),
            pl.BlockSpec((1, d, tm), lambda s, i: (s, 0, i)),
            pl.BlockSpec((1, d, tm), lambda s, i: (s, 0, i)),
            pl.BlockSpec((tm, d), row),
            pl.BlockSpec((tm, d), row),
        ],
        out_shape=[
            jax.ShapeDtypeStruct((t, d), BF16),
            jax.ShapeDtypeStruct((n_seq, d, seq), F32),
            jax.ShapeDtypeStruct((n_seq, d, seq), BF16),
            jax.ShapeDtypeStruct((t, d), F32),
            jax.ShapeDtypeStruct((t, d), BF16),
        ],
        compiler_params=_params(2),
        name="da_qkv_prompt",
    )(x, wq, wkt, wv, c, up, dn, ct, st)


def _diff_lambda(lq1_ref, lk1_ref, lq2_ref, lk2_ref):
    s1 = jnp.sum(lq1_ref[...] * lk1_ref[...], axis=-1, keepdims=True)
    s2 = jnp.sum(lq2_ref[...] * lk2_ref[...], axis=-1, keepdims=True)
    return jnp.exp(s1) - jnp.exp(s2) + DA_LAMBDA_INIT


def _flash_body(qi_ref, kj_ref, lq1_ref, lk1_ref, lq2_ref, lk2_ref, q_ref, kt_ref, v_ref, sg_ref,
                o_ref, q2_ref, m_ref, l_ref, acc_ref):
    step = pl.program_id(2)
    qi = qi_ref[step]
    kj = kj_ref[step]
    tq = q_ref.shape[0]
    tk = v_ref.shape[0]

    @pl.when(kj == 0)
    def _init():
        q = q_ref[...]
        lane = lax.broadcasted_iota(jnp.int32, q.shape, 1)
        zero = jnp.zeros_like(q)
        q2_ref[0:tq, :] = jnp.where(lane < DA_HD, q, zero)
        q2_ref[tq:, :] = jnp.where(lane >= DA_HD, q, zero)
        m_ref[...] = jnp.full(m_ref.shape, NEG_INF, F32)
        l_ref[...] = jnp.zeros(l_ref.shape, F32)
        acc_ref[...] = jnp.zeros(acc_ref.shape, F32)

    s = _mm(q2_ref[...], kt_ref[0])
    row = lax.broadcasted_iota(jnp.int32, s.shape, 0)
    col = lax.broadcasted_iota(jnp.int32, s.shape, 1)
    q_pos = qi * tq + jnp.where(row >= tq, row - tq, row)
    s = jnp.where(kj * tk + col <= q_pos, s, NEG_INF)
    m_prev = m_ref[...]
    m_new = jnp.maximum(m_prev, jnp.max(s, axis=-1, keepdims=True))
    alpha = jnp.exp(m_prev - m_new)
    p = jnp.exp(s - m_new)
    l_ref[...] = alpha * l_ref[...] + jnp.sum(p, axis=-1, keepdims=True)
    acc_ref[...] = alpha * acc_ref[...] + _mm(p.astype(BF16), v_ref[...])
    m_ref[...] = m_new

    @pl.when(kj == qi)
    def _finish():
        lam = _diff_lambda(lq1_ref, lk1_ref, lq2_ref, lk2_ref)
        o = acc_ref[...] / l_ref[...]
        o = o[0:tq, :] - lam * o[tq:, :]
        o_ref[...] = (_rms_norm(o, sg_ref[...]) * (1.0 - DA_LAMBDA_INIT)).astype(BF16)


def _flash_prompt(q, ktb, vb, n_seq, lams, subln_g):
    t, d = q.shape
    seq = t // n_seq
    ta = min(ATT_TILE, seq)
    nq = seq // ta
    hw = 2 * DA_HD
    pairs = [(i, j) for i in range(nq) for j in range(i + 1)]
    qi_tab = jnp.asarray([i for i, _ in pairs], jnp.int32)
    kj_tab = jnp.asarray([j for _, j in pairs], jnp.int32)
    small = lambda shape: pl.BlockSpec(shape, lambda s, h, p, qi, kj: (0,) * len(shape))
    grid_spec = pltpu.PrefetchScalarGridSpec(
        num_scalar_prefetch=2,
        grid=(n_seq, DA_HEADS, len(pairs)),
        in_specs=[small((1, DA_HD))] * 4 + [
            pl.BlockSpec((ta, hw), lambda s, h, p, qi, kj: (s * nq + qi[p], h)),
            pl.BlockSpec((1, hw, ta), lambda s, h, p, qi, kj: (s, h, kj[p])),
            pl.BlockSpec((ta, hw), lambda s, h, p, qi, kj: (s * nq + kj[p], h)),
            small((1, hw)),
        ],
        out_specs=pl.BlockSpec((ta, hw), lambda s, h, p, qi, kj: (s * nq + qi[p], h)),
        scratch_shapes=[
            pltpu.VMEM((2 * ta, hw), BF16),
            pltpu.VMEM((2 * ta, 1), F32),
            pltpu.VMEM((2 * ta, 1), F32),
            pltpu.VMEM((2 * ta, hw), F32),
        ],
    )
    return pl.pallas_call(
        _flash_body,
        grid_spec=grid_spec,
        out_shape=jax.ShapeDtypeStruct((t, d), BF16),
        compiler_params=_params(3),
        name="flash_prompt",
    )(qi_tab, kj_tab, *lams, q, ktb, vb, subln_g)


def _da_qkv_sample_body(x_ref, w_ref, c_ref, up_ref, dn_ref, q_ref, k_ref, v_ref):
    d = x_ref.shape[1]
    p = _mm(x_ref[...].astype(BF16), w_ref[...])
    c, up, dn = c_ref[...], up_ref[...], dn_ref[...]
    q_ref[...] = _rope_rows(p[:, :d], c, up, dn) * (DA_HD ** -0.5)
    k_ref[...] = _rope_rows(p[:, d:2 * d], c, up, dn)
    v_ref[...] = p[:, 2 * d:]


def _da_qkv_sample(x, w_in, tables):
    n, d = x.shape
    c, up, dn = tables[:3]
    return pl.pallas_call(
        _da_qkv_sample_body,
        out_shape=[jax.ShapeDtypeStruct((n, d), F32)] * 3,
        compiler_params=pltpu.CompilerParams(vmem_limit_bytes=VMEM_LIMIT),
        name="da_qkv_sample",
    )(x, w_in, c, up, dn)


def _lane_broadcast_columns(vec_bf, eye):
    w = vec_bf.shape[1]
    rows = jnp.broadcast_to(vec_bf, (LANES, w))
    return jnp.concatenate(
        [_mm_nt(eye, rows[:, i * LANES:(i + 1) * LANES]) for i in range(w // LANES)], axis=0)


def _identity_bf16():
    r = lax.broadcasted_iota(jnp.int32, (LANES, LANES), 0)
    c = lax.broadcasted_iota(jnp.int32, (LANES, LANES), 1)
    return jnp.where(r == c, 1.0, 0.0).astype(BF16)


def _decode_attn_body(pt_ref, lq1_ref, lk1_ref, lq2_ref, lk2_ref, q_ref, kn_ref, vn_ref,
                      kt_ref, v_ref, sg_ref, o_ref, qb_ref, m_ref, l_ref, acc_ref):
    del pt_ref
    j = pl.program_id(1)
    n_comp = 2 * DA_HEADS
    head_of_row = lax.broadcasted_iota(jnp.int32, (n_comp, LANES), 0) % DA_HEADS

    @pl.when(j == 0)
    def _init():
        qb = _lane_broadcast_columns(q_ref[0].astype(BF16), _identity_bf16())
        qb_ref[...] = qb.reshape(qb_ref.shape)
        m_ref[...] = jnp.full(m_ref.shape, NEG_INF, F32)
        l_ref[...] = jnp.zeros(l_ref.shape, F32)
        acc_ref[...] = jnp.zeros(acc_ref.shape, F32)

    def scores(kt_of_comp):
        parts = []
        for comp in range(2):
            qb = qb_ref[pl.ds(comp, DA_HEADS, stride=2)]
            parts.append(jnp.sum(qb * kt_of_comp(comp), axis=1))
        return jnp.concatenate(parts, axis=0)

    def online_update(s, pv_of_p):
        m_prev = m_ref[...]
        m_new = jnp.maximum(m_prev, jnp.max(s, axis=-1, keepdims=True))
        alpha = jnp.exp(m_prev - m_new)
        p = jnp.exp(s - m_new)
        l_ref[...] = alpha * l_ref[...] + jnp.sum(p, axis=-1, keepdims=True)
        acc_ref[...] = alpha * acc_ref[...] + pv_of_p(p)
        m_ref[...] = m_new

    def page_pv(p):
        out = jnp.zeros((n_comp, LANES), F32)
        for h in range(DA_HEADS):
            ph = jnp.where(head_of_row == h, p, 0.0).astype(BF16)
            out = out + _mm(ph, v_ref[0, :, h, :].astype(BF16))
        return out

    s_page = scores(lambda comp: kt_ref[0, pl.ds(comp, DA_HEADS, stride=2)])
    online_update(s_page, page_pv)

    @pl.when(j == pl.num_programs(1) - 1)
    def _finish():
        kb = _lane_broadcast_columns(kn_ref[0].astype(BF16), _identity_bf16())
        kb = kb.reshape(DA_HEADS, 2, DA_HD, LANES)
        s_self = jnp.concatenate(
            [jnp.sum(qb_ref[pl.ds(comp, DA_HEADS, stride=2)] * kb[:, comp], axis=1)
             for comp in range(2)], axis=0)
        lane = lax.broadcasted_iota(jnp.int32, (n_comp, LANES), 1)
        s_self = jnp.where(lane == 0, s_self, NEG_INF)
        vn = vn_ref[0]
        v_rows = jnp.concatenate(
            [vn[:, h * LANES:(h + 1) * LANES] for h in range(DA_HEADS)] * 2, axis=0)
        online_update(s_self, lambda p: jnp.sum(p, axis=-1, keepdims=True) * v_rows)
        lam = _diff_lambda(lq1_ref, lk1_ref, lq2_ref, lk2_ref)
        o = acc_ref[...] / l_ref[...]
        o = o[0:DA_HEADS, :] - lam * o[DA_HEADS:, :]
        o_ref[0] = _rms_norm(o, sg_ref[...]) * (1.0 - DA_LAMBDA_INIT)


def _decode_attn(q, k_new, v_new, cache_kt, cache_v, page_table, lams, subln_g):
    n, d = q.shape
    n_pages = page_table.shape[1]
    n_comp = 2 * DA_HEADS
    hw = 2 * DA_HD
    row3 = lambda a: a.reshape(n, 1, d)
    vec_spec = pl.BlockSpec((1, 1, d), lambda s, j, pt: (s, 0, 0))
    small = lambda shape: pl.BlockSpec(shape, lambda s, j, pt: (0,) * len(shape))
    grid_spec = pltpu.PrefetchScalarGridSpec(
        num_scalar_prefetch=1,
        grid=(n, n_pages),
        in_specs=[small((1, DA_HD))] * 4 + [
            vec_spec, vec_spec, vec_spec,
            pl.BlockSpec((1, n_comp, DA_HD, PAGE_SIZE),
                         lambda s, j, pt: (pt[s * n_pages + j], 0, 0, 0)),
            pl.BlockSpec((1, PAGE_SIZE, DA_HEADS, hw),
                         lambda s, j, pt: (pt[s * n_pages + j], 0, 0, 0)),
            small((1, hw)),
        ],
        out_specs=pl.BlockSpec((1, DA_HEADS, hw), lambda s, j, pt: (s, 0, 0)),
        scratch_shapes=[
            pltpu.VMEM((n_comp, DA_HD, LANES), F32),
            pltpu.VMEM((n_comp, 1), F32),
            pltpu.VMEM((n_comp, 1), F32),
            pltpu.VMEM((n_comp, hw), F32),
        ],
    )
    return pl.pallas_call(
        _decode_attn_body,
        grid_spec=grid_spec,
        out_shape=jax.ShapeDtypeStruct((n, DA_HEADS, hw), F32),
        compiler_params=_params(2),
        name="decode_attn",
    )(page_table.reshape(-1), *lams, row3(q), row3(k_new), row3(v_new), cache_kt, cache_v, subln_g)


def _cumsum_rows(g, tri):
    hi = g.astype(BF16)
    r1 = g - hi.astype(F32)
    mid = r1.astype(BF16)
    lo = (r1 - mid.astype(F32)).astype(BF16)
    return _mm(tri, hi) + _mm(tri, mid) + _mm(tri, lo)


def _chunk_recurrence(q, k, v, g, st_ref, n_heads, dk, dv, post):
    c = q.shape[0]
    row = lax.broadcasted_iota(jnp.int32, (c, c), 0)
    col = lax.broadcasted_iota(jnp.int32, (c, c), 1)
    causal = row >= col
    big_g = _cumsum_rows(g, jnp.where(causal, 1.0, 0.0).astype(BF16))
    g_mid = big_g[c // 2 - 1:c // 2, :]
    g_last = big_g[c - 1:c, :]
    q_state = (q * jnp.exp(big_g)).astype(BF16)
    q_dec = (q * jnp.exp(big_g - g_mid)).astype(BF16)
    k_dec = (k * jnp.exp(g_mid - big_g)).astype(BF16)
    k_tail = (k * jnp.exp(g_last - big_g)).astype(BF16)
    decay = jnp.exp(g_last)
    vb = v.astype(BF16)
    outs = []
    for h in range(n_heads):
        ks = slice(h * dk, (h + 1) * dk)
        vs = slice(h * dv, (h + 1) * dv)
        scores = jnp.where(causal, _mm_nt(q_dec[:, ks], k_dec[:, ks]), 0.0)
        state = st_ref[h]
        o = _mm_nt(q_state[:, ks], state.astype(BF16)) + _mm(scores.astype(BF16), vb[:, vs])
        st_ref[h] = state * decay[:, ks] + _mm_tn(vb[:, vs], k_tail[:, ks])
        outs.append(post(o, h))
    return jnp.concatenate(outs, axis=1)


def _hgrn_lower_bound(lbl_ref, layer):
    logits = lbl_ref[...]
    e = jnp.exp(logits - jnp.max(logits, axis=0, keepdims=True))
    p = e / jnp.sum(e, axis=0, keepdims=True)
    return jnp.sum(p[0:layer + 1, :], axis=0, keepdims=True) - p[0:1, :]


def _hgrn_gates(p, lb, d):
    forget = lb + (1.0 - lb) * jax.nn.sigmoid(p[:, d:2 * d])
    return _silu(p[:, :d]), 1.0 - forget, p[:, 2 * d:], jnp.log(forget)


def _gla_gates(p, gk, wgk2_ref, bgk2_ref, d):
    key = d // 2
    z = _mm(gk.astype(BF16), wgk2_ref[...]) + bgk2_ref[...]
    log_sig = jnp.minimum(z, 0.0) - jnp.log(1.0 + jnp.exp(-jnp.abs(z)))
    q = p[:, :key] * ((key // GLA_HEADS) ** -0.5)
    return q, p[:, key:2 * key], p[:, 2 * key:2 * key + d], log_sig / GLA_GATE_NORM


def _recurrent_prompt_body(kind, layer, *refs):
    if kind == "hgrn":
        (x_ref, win_ref, lbl_ref, ng_ref, wout_ref, g_ref, b_ref,
         o_ref, sfin_ref, st_ref) = refs
    else:
        (x_ref, win_ref, wgk_ref, wgk2_ref, bgk2_ref, ng_ref, wout_ref, g_ref, b_ref,
         o_ref, sfin_ref, st_ref) = refs
    t = pl.program_id(1)
    tm, d = x_ref.shape
    n_heads, dv, dk = st_ref.shape

    @pl.when(t == 0)
    def _reset():
        st_ref[...] = jnp.zeros(st_ref.shape, F32)

    x = x_ref[...]
    xb = x.astype(BF16)
    p = _mm(xb, win_ref[...])
    ng = ng_ref[...]
    if kind == "hgrn":
        q, k, v, g = _hgrn_gates(p, _hgrn_lower_bound(lbl_ref, layer), d)
        post = lambda o, h: _rms_norm(o, ng)
    else:
        q, k, v, g = _gla_gates(p, _mm(xb, wgk_ref[...]), wgk2_ref, bgk2_ref, d)
        gate = p[:, 2 * d:3 * d]
        post = None
    chunk = min(REC_CHUNK, tm)
    outs = []
    for ci in range(tm // chunk):
        rows = slice(ci * chunk, (ci + 1) * chunk)
        if kind == "gla":
            gate_c = gate[rows]
            post = lambda o, h, gate_c=gate_c: _rms_norm(o, ng) * _silu(gate_c[:, h * dv:(h + 1) * dv])
        outs.append(_chunk_recurrence(q[rows], k[rows], v[rows], g[rows], st_ref,
                                      n_heads, dk, dv, post))
    on = jnp.concatenate(outs, axis=0)
    y = _mm(on.astype(BF16), wout_ref[...])
    o_ref[...] = _layer_norm(ALPHA * x + y, g_ref[...], b_ref[...])

    @pl.when(t == pl.num_programs(1) - 1)
    def _state():
        for h in range(n_heads):
            sfin_ref[0, h] = st_ref[h].T


def _recurrent_prompt(kind, layer, x, n_seq, n_heads, dk, dv, weights, w_out, g, b):
    t, d = x.shape
    seq = t // n_seq
    tm = min(MIX_TILE, seq)
    nt = seq // tm
    row = lambda s, i: (s * nt + i, 0)
    w_specs = [_const_spec(w.shape) for w in weights]
    return pl.pallas_call(
        functools.partial(_recurrent_prompt_body, kind, layer),
        grid=(n_seq, nt),
        in_specs=[pl.BlockSpec((tm, d), row)] + w_specs + [
            _const_spec((d, d)), _const_spec((1, d)), _const_spec((1, d))],
        out_specs=[
            pl.BlockSpec((tm, d), row),
            pl.BlockSpec((1, n_heads, dk, dv), lambda s, i: (s, 0, 0, 0)),
        ],
        out_shape=[
            jax.ShapeDtypeStruct((t, d), F32),
            jax.ShapeDtypeStruct((n_seq, n_heads, dk, dv), F32),
        ],
        scratch_shapes=[pltpu.VMEM((n_heads, dv, dk), F32)],
        compiler_params=_params(2),
        name=kind + "_prompt",
    )(x, *weights, w_out, g, b)


def _recurrent_sample_body(kind, layer, *refs):
    if kind == "hgrn":
        (x_ref, s_ref, win_ref, lbl_ref, ng_ref, wout_ref, g_ref, b_ref,
         o_ref, snew_ref, q_scr, k_scr, v_scr, f_scr, gate_scr, on_scr) = refs
    else:
        (x_ref, s_ref, win_ref, wgk_ref, wgk2_ref, bgk2_ref, ng_ref, wout_ref, g_ref, b_ref,
         o_ref, snew_ref, q_scr, k_scr, v_scr, f_scr, gate_scr, on_scr) = refs
    n = pl.program_id(0)
    d = x_ref.shape[1]
    _, n_heads, dk, dv = s_ref.shape

    @pl.when(n == 0)
    def _project():
        xb = x_ref[...].astype(BF16)
        p = _mm(xb, win_ref[...])
        if kind == "hgrn":
            q, k, v, g = _hgrn_gates(p, _hgrn_lower_bound(lbl_ref, layer), d)
            gate_scr[...] = jnp.zeros(gate_scr.shape, F32)
        else:
            q, k, v, g = _gla_gates(p, _mm(xb, wgk_ref[...]), wgk2_ref, bgk2_ref, d)
            gate_scr[...] = p[:, 2 * d:3 * d]
        q_scr[...] = q
        k_scr[...] = k
        v_scr[...] = v
        f_scr[...] = jnp.exp(g)

    eye = _identity_bf16()
    q = q_scr[pl.ds(n, 1), :]
    k = k_scr[pl.ds(n, 1), :]
    v = v_scr[pl.ds(n, 1), :]
    f = f_scr[pl.ds(n, 1), :]
    f_hi = f.astype(BF16)
    f_r = f - f_hi.astype(F32)
    f_mid = f_r.astype(BF16)
    f_lo = (f_r - f_mid.astype(F32)).astype(BF16)
    f_col = (_lane_broadcast_columns(f_hi, eye) + _lane_broadcast_columns(f_mid, eye)
             + _lane_broadcast_columns(f_lo, eye))
    k_col = _lane_broadcast_columns(k.astype(BF16), eye)
    reps = dv // LANES
    widen = lambda a: a if reps == 1 else jnp.concatenate([a] * reps, axis=1)
    outs = []
    for h in range(n_heads):
        ks = slice(h * dk, (h + 1) * dk)
        vs = slice(h * dv, (h + 1) * dv)
        s_new = widen(f_col[ks]) * s_ref[0, h] + widen(k_col[ks]) * v[:, vs]
        snew_ref[0, h] = s_new
        q_rows = jnp.broadcast_to(q[:, ks], (2 * SUBLANES, dk)).astype(BF16)
        outs.append(_mm(q_rows, s_new.astype(BF16))[0:1, :])
    on_scr[pl.ds(n, 1), :] = jnp.concatenate(outs, axis=1)

    @pl.when(n == pl.num_programs(0) - 1)
    def _finish():
        ng = ng_ref[...]
        o = on_scr[...]
        parts = []
        for h in range(n_heads):
            oh = _rms_norm(o[:, h * dv:(h + 1) * dv], ng)
            if kind == "gla":
                oh = oh * _silu(gate_scr[:, h * dv:(h + 1) * dv])
            parts.append(oh)
        y = _mm(jnp.concatenate(parts, axis=1).astype(BF16), wout_ref[...])
        o_ref[...] = _layer_norm(ALPHA * x_ref[...] + y, g_ref[...], b_ref[...])


def _recurrent_sample(kind, layer, x, state, weights, w_out, g, b):
    n, d = x.shape
    _, n_heads, dk, dv = state.shape
    w_specs = [_const_spec(w.shape) for w in weights]
    st_spec = pl.BlockSpec((1, n_heads, dk, dv), lambda s: (s, 0, 0, 0))
    return pl.pallas_call(
        functools.partial(_recurrent_sample_body, kind, layer),
        grid=(n,),
        in_specs=[_const_spec((n, d)), st_spec] + w_specs + [
            _const_spec((d, d)), _const_spec((1, d)), _const_spec((1, d))],
        out_specs=[_const_spec((n, d)), st_spec],
        out_shape=[jax.ShapeDtypeStruct((n, d), F32), jax.ShapeDtypeStruct(state.shape, F32)],
        scratch_shapes=[
            pltpu.VMEM((n, n_heads * dk), F32),
            pltpu.VMEM((n, n_heads * dk), F32),
            pltpu.VMEM((n, n_heads * dv), F32),
            pltpu.VMEM((n, n_heads * dk), F32),
            pltpu.VMEM((n, d), F32),
            pltpu.VMEM((n, n_heads * dv), F32),
        ],
        compiler_params=_params(1),
        name=kind + "_sample",
    )(x, state, *weights, w_out, g, b)


def kernel(x_prompt, x_sample, state_conv, cache_k, cache_v, page_table, state_hgrn, state_gla,
           ffn_w_up, ffn_w_down, ln_g, ln_b,
           conv_w_in, conv_w, conv_w_out,
           da_w_in, da_lambda_q1, da_lambda_k1, da_lambda_q2, da_lambda_k2, da_subln_g, da_w_out,
           hg_w_in, hg_lb_logits, hg_norm_g, hg_w_out,
           gla_w_in, gla_w_gk2, gla_b_gk2, gla_norm_g, gla_w_out):
    n_seq, seq, d = x_prompt.shape
    n_smp = x_sample.shape[0]
    xp = x_prompt.reshape(n_seq * seq, d)
    xs = x_sample.reshape(n_smp, d)
    bf = lambda w: w.astype(BF16)
    row = lambda v: v.reshape(1, -1)
    w_up, w_down = bf(ffn_w_up), bf(ffn_w_down)

    def ffn_pair(i, half, xp, xs):
        g, b = row(ln_g[i, half * 2]), row(ln_b[i, half * 2])
        return (_ffn(xp, w_up[i, half], w_down[i, half], g, b),
                _ffn(xs, w_up[i, half], w_down[i, half], g, b))

    xp, xs = ffn_pair(0, 0, xp, xs)
    g, b = row(ln_g[0, 1]), row(ln_b[0, 1])
    cw_in, cw_out = bf(conv_w_in), bf(conv_w_out)
    xp, conv_p = _conv_prompt(xp, n_seq, cw_in, conv_w, cw_out, g, b)
    xs, u_s = _conv_sample(xs, state_conv[:, 0], state_conv[:, 1], cw_in, conv_w, cw_out, g, b)
    conv_s = jnp.stack([state_conv[:, 1], u_s], axis=1)
    xp, xs = ffn_pair(0, 1, xp, xs)

    xp, xs = ffn_pair(1, 0, xp, xs)
    g, b = row(ln_g[1, 1]), row(ln_b[1, 1])
    lams = [row(v) for v in (da_lambda_q1, da_lambda_k1, da_lambda_q2, da_lambda_k2)]
    subln = row(da_subln_g)
    dw_in, dw_out = bf(da_w_in), bf(da_w_out)
    past = page_table.shape[1] * PAGE_SIZE
    q, kt, ktb, v, vb = _da_qkv_prompt(
        xp, n_seq, dw_in[:, :d], dw_in[:, d:2 * d].T, dw_in[:, 2 * d:],
        _rope_tables(jnp.arange(seq, dtype=jnp.int32)))
    on = _flash_prompt(q, ktb, vb, n_seq, lams, subln)
    k_p = kt.reshape(n_seq, 2 * DA_HEADS, DA_HD, seq).transpose(0, 3, 1, 2)
    v_p = v.reshape(n_seq, seq, DA_HEADS, 2 * DA_HD)
    xp = _proj_norm(on, xp, dw_out, g, b)
    qs, ks, vs = _da_qkv_sample(xs, dw_in, _rope_tables(jnp.full((1,), past, jnp.int32)))
    ons = _decode_attn(qs, ks, vs, cache_k.transpose(0, 2, 3, 1), cache_v, page_table, lams, subln)
    k_s = ks.reshape(n_smp, 1, 2 * DA_HEADS, DA_HD)
    v_s = vs.reshape(n_smp, 1, DA_HEADS, 2 * DA_HD)
    xs = _proj_norm(ons.reshape(n_smp, d).astype(BF16), xs, dw_out, g, b)
    xp, xs = ffn_pair(1, 1, xp, xs)

    xp, xs = ffn_pair(2, 0, xp, xs)
    g, b = row(ln_g[2, 1]), row(ln_b[2, 1])
    hg_weights = (bf(hg_w_in), hg_lb_logits, row(hg_norm_g))
    dk = d // HG_HEADS
    xp, hg_p = _recurrent_prompt("hgrn", 2, xp, n_seq, HG_HEADS, dk, dk, hg_weights,
                                 bf(hg_w_out), g, b)
    xs, hg_s = _recurrent_sample("hgrn", 2, xs, state_hgrn, hg_weights, bf(hg_w_out), g, b)
    xp, xs = ffn_pair(2, 1, xp, xs)

    xp, xs = ffn_pair(3, 0, xp, xs)
    g, b = row(ln_g[3, 1]), row(ln_b[3, 1])
    main = 3 * d
    w_gk = jnp.pad(bf(gla_w_in[:, main:]), ((0, 0), (0, LANES - GLA_LOWRANK)))
    w_gk2 = jnp.pad(bf(gla_w_gk2), ((0, LANES - GLA_LOWRANK), (0, 0)))
    gla_weights = (bf(gla_w_in[:, :main]), w_gk, w_gk2, row(gla_b_gk2), row(gla_norm_g))
    gdk = d // 2 // GLA_HEADS
    gdv = d // GLA_HEADS
    xp, gla_p = _recurrent_prompt("gla", 3, xp, n_seq, GLA_HEADS, gdk, gdv, gla_weights,
                                  bf(gla_w_out), g, b)
    xs, gla_s = _recurrent_sample("gla", 3, xs, state_gla, gla_weights, bf(gla_w_out), g, b)
    xp, xs = ffn_pair(3, 1, xp, xs)

    return (xp.reshape(n_seq, seq, d), xs.reshape(n_smp, 1, d), conv_p, conv_s,
            k_p, v_p, k_s, v_s, hg_p, hg_s, gla_p, gla_s)
```

```python
import functools
import math

import jax
import jax.numpy as jnp
from jax import lax
from jax.experimental import pallas as pl
from jax.experimental.pallas import tpu as pltpu

F32 = jnp.float32
BF16 = jnp.bfloat16

DEPTH = 4
LN_EPS = 1e-5
RMS_EPS = 1e-6
ALPHA = (2.0 * DEPTH) ** 0.25
CONV_W = 3
DA_HEADS = 8
DA_HD = 64
ROT_DIM = DA_HD // 4
ROPE_THETA = 500000.0
DA_LAMBDA_INIT = 0.8 - 0.6 * math.exp(-0.3 * 1)
NEG_INF = -1e30
PAGE_SIZE = 128
HG_HEADS = 8
GLA_HEADS = 4
GLA_LOWRANK = 16
GLA_GATE_NORM = 16.0

LANES = 128
SUBLANES = 8
VMEM_LIMIT = 56 * 1024 * 1024

TOKEN_TILE = 512
FFN_CHUNK = 1408
MIX_TILE = 256
REC_CHUNK = 64
ATT_TILE = 512
DECODE_PAGES_PER_STEP = 8


def _mm(a, b):
    return jnp.dot(a, b, preferred_element_type=F32)


def _mm_nt(a, b):
    return lax.dot_general(a, b, (((1,), (1,)), ((), ())), preferred_element_type=F32)


def _mm_tn(a, b):
    return lax.dot_general(a, b, (((0,), (0,)), ((), ())), preferred_element_type=F32)


def _layer_norm(z, g, b):
    mu = jnp.mean(z, axis=-1, keepdims=True)
    zc = z - mu
    var = jnp.mean(zc * zc, axis=-1, keepdims=True)
    return zc * lax.rsqrt(var + LN_EPS) * g + b


def _rms_norm(o, g):
    return o * lax.rsqrt(jnp.mean(o * o, axis=-1, keepdims=True) + RMS_EPS) * g


def _silu(a):
    return a * jax.nn.sigmoid(a)


def _params(n_axes):
    return pltpu.CompilerParams(dimension_semantics=("arbitrary",) * n_axes,
                                vmem_limit_bytes=VMEM_LIMIT)


def _const_spec(shape):
    nd = len(shape)
    return pl.BlockSpec(shape, lambda *_: (0,) * nd)


def _ffn_body(x_ref, wa_ref, wu_ref, wd_ref, g_ref, b_ref, o_ref, xb_ref):
    j = pl.program_id(1)

    @pl.when(j == 0)
    def _init():
        xb_ref[...] = x_ref[...].astype(BF16)
        o_ref[...] = jnp.zeros_like(o_ref)

    xb = xb_ref[...]
    a = _mm(xb, wa_ref[...])
    u = _mm(xb, wu_ref[...])
    h = (_silu(a) * u).astype(BF16)
    o_ref[...] += _mm(h, wd_ref[...])

    @pl.when(j == pl.num_programs(1) - 1)
    def _finish():
        z = ALPHA * x_ref[...] + 0.5 * o_ref[...]
        o_ref[...] = _layer_norm(z, g_ref[...], b_ref[...])


def _ffn(x, w_up, w_down, layer, half, g, b):
    t, d = x.shape
    f = w_down.shape[2]
    tm = min(TOKEN_TILE, t)
    fc = FFN_CHUNK if f % FFN_CHUNK == 0 else f
    nf = f // fc
    return pl.pallas_call(
        _ffn_body,
        grid=(t // tm, nf),
        in_specs=[
            pl.BlockSpec((tm, d), lambda i, j: (i, 0)),
            pl.BlockSpec((None, None, d, fc), lambda i, j: (layer, half, 0, j)),
            pl.BlockSpec((None, None, d, fc), lambda i, j: (layer, half, 0, j + nf)),
            pl.BlockSpec((None, None, fc, d), lambda i, j: (layer, half, j, 0)),
            _const_spec((1, d)),
            _const_spec((1, d)),
        ],
        out_specs=pl.BlockSpec((tm, d), lambda i, j: (i, 0)),
        out_shape=jax.ShapeDtypeStruct((t, d), F32),
        scratch_shapes=[pltpu.VMEM((tm, d), BF16)],
        compiler_params=_params(2),
        name="ffn",
    )(x, w_up, w_up, w_down, g, b)


def _proj_norm_body(y_ref, x_ref, w_ref, g_ref, b_ref, o_ref):
    y = _mm(y_ref[...], w_ref[...])
    o_ref[...] = _layer_norm(ALPHA * x_ref[...] + y, g_ref[...], b_ref[...])


def _proj_norm(y, x, w, g, b):
    t, d = x.shape
    tm = min(TOKEN_TILE, t)
    return pl.pallas_call(
        _proj_norm_body,
        grid=(t // tm,),
        in_specs=[
            pl.BlockSpec((tm, d), lambda i: (i, 0)),
            pl.BlockSpec((tm, d), lambda i: (i, 0)),
            _const_spec((d, d)),
            _const_spec((1, d)),
            _const_spec((1, d)),
        ],
        out_specs=pl.BlockSpec((tm, d), lambda i: (i, 0)),
        out_shape=jax.ShapeDtypeStruct((t, d), F32),
        compiler_params=_params(1),
        name="proj_norm",
    )(y, x, w, g, b)


def _conv_prompt_body(x_ref, win_ref, cw_ref, wout_ref, g_ref, b_ref, o_ref, st_ref, ubuf_ref):
    t = pl.program_id(1)
    tm, d = x_ref.shape

    @pl.when(t == 0)
    def _reset():
        ubuf_ref[0:SUBLANES, :] = jnp.zeros((SUBLANES, d), F32)

    x = x_ref[...]
    p = _mm(x.astype(BF16), win_ref[...])
    u = p[:, d:2 * d] * p[:, 2 * d:]
    ubuf_ref[SUBLANES:, :] = u
    ue = ubuf_ref[...]
    um1 = pltpu.roll(ue, 1, 0)[SUBLANES:, :]
    um2 = pltpu.roll(ue, 2, 0)[SUBLANES:, :]
    cw = cw_ref[...]
    conv = cw[0:1, :] * um2 + cw[1:2, :] * um1 + cw[2:3, :] * u
    y = _mm((p[:, :d] * conv).astype(BF16), wout_ref[...])
    o_ref[...] = _layer_norm(ALPHA * x + y, g_ref[...], b_ref[...])
    ubuf_ref[0:SUBLANES, :] = ubuf_ref[tm:tm + SUBLANES, :]

    @pl.when(t == pl.num_programs(1) - 1)
    def _state():
        st_ref[0] = ubuf_ref[tm + SUBLANES - (CONV_W - 1):tm + SUBLANES, :]


def _conv_prompt(x, n_seq, w_in, conv_w, w_out, g, b):
    t, d = x.shape
    seq = t // n_seq
    tm = min(TOKEN_TILE, seq)
    nt = seq // tm
    return pl.pallas_call(
        _conv_prompt_body,
        grid=(n_seq, nt),
        in_specs=[
            pl.BlockSpec((tm, d), lambda s, i: (s * nt + i, 0)),
            _const_spec((d, 3 * d)),
            _const_spec((CONV_W, d)),
            _const_spec((d, d)),
            _const_spec((1, d)),
            _const_spec((1, d)),
        ],
        out_specs=[
            pl.BlockSpec((tm, d), lambda s, i: (s * nt + i, 0)),
            pl.BlockSpec((1, CONV_W - 1, d), lambda s, i: (s, 0, 0)),
        ],
        out_shape=[
            jax.ShapeDtypeStruct((t, d), F32),
            jax.ShapeDtypeStruct((n_seq, CONV_W - 1, d), F32),
        ],
        scratch_shapes=[pltpu.VMEM((tm + SUBLANES, d), F32)],
        compiler_params=_params(2),
        name="conv_prompt",
    )(x, w_in, conv_w, w_out, g, b)


def _conv_sample_body(x_ref, s0_ref, s1_ref, win_ref, cw_ref, wout_ref, g_ref, b_ref,
                      o_ref, u_ref):
    d = x_ref.shape[1]
    x = x_ref[...]
    p = _mm(x.astype(BF16), win_ref[...])
    u = p[:, d:2 * d] * p[:, 2 * d:]
    cw = cw_ref[...]
    conv = cw[0:1, :] * s0_ref[...] + cw[1:2, :] * s1_ref[...] + cw[2:3, :] * u
    y = _mm((p[:, :d] * conv).astype(BF16), wout_ref[...])
    o_ref[...] = _layer_norm(ALPHA * x + y, g_ref[...], b_ref[...])
    u_ref[...] = u


def _conv_sample(x, s0, s1, w_in, conv_w, w_out, g, b):
    n, d = x.shape
    return pl.pallas_call(
        _conv_sample_body,
        out_shape=[jax.ShapeDtypeStruct((n, d), F32), jax.ShapeDtypeStruct((n, d), F32)],
        compiler_params=pltpu.CompilerParams(vmem_limit_bytes=VMEM_LIMIT),
        name="conv_sample",
    )(x, s0, s1, w_in, conv_w, w_out, g, b)


def _rope_tables(pos):
    inv_freq = 1.0 / (ROPE_THETA ** (jnp.arange(0, ROT_DIM, 2, dtype=F32) / ROT_DIM))
    ang = pos.astype(F32)[:, None] * inv_freq[None, :]
    cos, sin = jnp.cos(ang), jnp.sin(ang)
    half = ROT_DIM // 2
    n = pos.shape[0]
    rest = DA_HD - ROT_DIM
    c64 = jnp.concatenate([cos, cos, jnp.ones((n, rest), F32)], axis=1)
    up64 = jnp.concatenate([-sin, jnp.zeros((n, DA_HD - half), F32)], axis=1)
    dn64 = jnp.concatenate([jnp.zeros((n, half), F32), sin, jnp.zeros((n, rest), F32)], axis=1)
    rep = LANES // DA_HD
    return (jnp.tile(c64, (1, rep)), jnp.tile(up64, (1, rep)), jnp.tile(dn64, (1, rep)),
            cos.T, sin.T)


def _rope_rows(x, c, s_up, s_dn):
    half = ROT_DIM // 2
    blocks = []
    for i in range(x.shape[1] // LANES):
        xb = x[:, i * LANES:(i + 1) * LANES]
        blocks.append(xb * c + pltpu.roll(xb, LANES - half, 1) * s_up + pltpu.roll(xb, half, 1) * s_dn)
    return jnp.concatenate(blocks, axis=1)


def _rope_cols(xt, ct, st):
    d, cols = xt.shape
    half = ROT_DIM // 2
    x3 = xt.reshape(d // DA_HD, DA_HD, cols)
    x1 = x3[:, 0:half, :]
    x2 = x3[:, half:ROT_DIM, :]
    x3 = jnp.concatenate([x1 * ct - x2 * st, x2 * ct + x1 * st, x3[:, ROT_DIM:, :]], axis=1)
    return x3.reshape(d, cols)


def _da_qkv_prompt_body(x_ref, wqt_ref, wkt_ref, wv_ref, ct_ref, st_ref,
                        qt_ref, kt_ref, kb_ref, v_ref, vt_ref):
    xb = x_ref[...].astype(BF16)
    ct = ct_ref[...]
    st = st_ref[...]
    qt = _rope_cols(_mm_nt(wqt_ref[...], xb), ct, st)
    qt_ref[0] = (qt * (DA_HD ** -0.5)).astype(BF16)
    kt = _rope_cols(_mm_nt(wkt_ref[...], xb), ct, st)
    kt_ref[0] = kt
    kb_ref[...] = kt.T.astype(BF16)
    v = _mm(xb, wv_ref[...])
    v_ref[...] = v
    vt_ref[0, 0] = v.T.astype(BF16)


def _da_qkv_prompt(x, n_seq, wqt, wkt, wv, tables):
    t, d = x.shape
    seq = t // n_seq
    tm = min(ATT_TILE, seq)
    nt = seq // tm
    ct, st = tables[3:]
    half = ROT_DIM // 2
    row = lambda s, i: (s * nt + i, 0)
    return pl.pallas_call(
        _da_qkv_prompt_body,
        grid=(n_seq, nt),
        in_specs=[
            pl.BlockSpec((tm, d), row),
            _const_spec((d, d)),
            _const_spec((d, d)),
            _const_spec((d, d)),
            pl.BlockSpec((half, tm), lambda s, i: (0, i)),
            pl.BlockSpec((half, tm), lambda s, i: (0, i)),
        ],
        out_specs=[
            pl.BlockSpec((1, d, tm), lambda s, i: (s, 0, i)),
            pl.BlockSpec((1, d, tm), lambda s, i: (s, 0, i)),
            pl.BlockSpec((tm, d), row),
            pl.BlockSpec((tm, d), row),
            pl.BlockSpec((1, 1, d, tm), lambda s, i: (s, i, 0, 0)),
        ],
        out_shape=[
            jax.ShapeDtypeStruct((n_seq, d, seq), BF16),
            jax.ShapeDtypeStruct((n_seq, d, seq), F32),
            jax.ShapeDtypeStruct((t, d), BF16),
            jax.ShapeDtypeStruct((t, d), F32),
            jax.ShapeDtypeStruct((n_seq, nt, d, tm), BF16),
        ],
        compiler_params=_params(2),
        name="da_qkv_prompt",
    )(x, wqt, wkt, wv, ct, st)


def _diff_lambda(lq1_ref, lk1_ref, lq2_ref, lk2_ref):
    s1 = jnp.sum(lq1_ref[...] * lk1_ref[...], axis=-1, keepdims=True)
    s2 = jnp.sum(lq2_ref[...] * lk2_ref[...], axis=-1, keepdims=True)
    return jnp.exp(s1) - jnp.exp(s2) + DA_LAMBDA_INIT


def _flash_body(lq1_ref, lk1_ref, lq2_ref, lk2_ref, qt_ref, k_ref, vt_ref, sg_ref, o_ref,
                q2_ref, m_ref, l_ref, acc_ref):
    qi = pl.program_id(2)
    tq = qt_ref.shape[2]
    tk = vt_ref.shape[3]
    n_col = 2 * tq
    cb = min(2 * LANES, tq)

    qt = qt_ref[0]
    sub = lax.broadcasted_iota(jnp.int32, qt.shape, 0)
    zero = jnp.zeros_like(qt)
    q2_ref[:, 0:tq] = jnp.where(sub < DA_HD, qt, zero)
    q2_ref[:, tq:] = jnp.where(sub >= DA_HD, qt, zero)
    m_ref[...] = jnp.full(m_ref.shape, NEG_INF, F32)
    l_ref[...] = jnp.zeros(l_ref.shape, F32)
    acc_ref[...] = jnp.zeros(acc_ref.shape, F32)

    def kv_tile(kj, diagonal):
        k = k_ref[pl.ds(pl.multiple_of(kj * tk, tk), tk), :]
        vt = vt_ref[0, kj]
        starts = list(range(0, n_col, cb))
        score = lambda c0: _mm(k, q2_ref[:, c0:c0 + cb])
        st_next = score(starts[0])
        for i, c0 in enumerate(starts):
            cs = slice(c0, c0 + cb)
            st = st_next
            if i + 1 < len(starts):
                st_next = score(starts[i + 1])
            if diagonal:
                key = lax.broadcasted_iota(jnp.int32, st.shape, 0)
                qry = lax.broadcasted_iota(jnp.int32, st.shape, 1) + (c0 % tq)
                st = jnp.where(key <= qry, st, NEG_INF)
            m_prev = m_ref[:, cs]
            m_new = jnp.maximum(m_prev, jnp.max(st, axis=0, keepdims=True))
            alpha = jnp.exp(m_prev - m_new)
            pt = jnp.exp(st - m_new)
            l_ref[:, cs] = alpha * l_ref[:, cs] + jnp.sum(pt, axis=0, keepdims=True)
            acc_ref[:, cs] = alpha * acc_ref[:, cs] + _mm(vt, pt.astype(BF16))
            m_ref[:, cs] = m_new

    def full_tile(kj, carry):
        kv_tile(kj, False)
        return carry

    lax.fori_loop(0, qi, full_tile, 0)
    kv_tile(qi, True)

    lam = _diff_lambda(lq1_ref, lk1_ref, lq2_ref, lk2_ref)
    o = acc_ref[...] / l_ref[...]
    o = o[:, 0:tq] - lam * o[:, tq:]
    r = lax.rsqrt(jnp.mean(o * o, axis=0, keepdims=True) + RMS_EPS)
    o = o * r * sg_ref[...] * (1.0 - DA_LAMBDA_INIT)
    o_ref[...] = o.T.astype(BF16)


def _flash_prompt(qt, kb, vt, lams, subln_col):
    n_seq, d, seq = qt.shape
    nq, ta = vt.shape[1], vt.shape[3]
    hw = 2 * DA_HD
    return pl.pallas_call(
        _flash_body,
        grid=(n_seq, DA_HEADS, nq),
        in_specs=[_const_spec((1, DA_HD))] * 4 + [
            pl.BlockSpec((1, hw, ta), lambda s, h, i: (s, h, i)),
            pl.BlockSpec((seq, hw), lambda s, h, i: (s, h)),
            pl.BlockSpec((1, nq, hw, ta), lambda s, h, i: (s, 0, h, 0)),
            _const_spec((hw, 1)),
        ],
        out_specs=pl.BlockSpec((ta, hw), lambda s, h, i: (s * nq + i, h)),
        out_shape=jax.ShapeDtypeStruct((n_seq * seq, d), BF16),
        scratch_shapes=[
            pltpu.VMEM((hw, 2 * ta), BF16),
            pltpu.VMEM((1, 2 * ta), F32),
            pltpu.VMEM((1, 2 * ta), F32),
            pltpu.VMEM((hw, 2 * ta), F32),
        ],
        compiler_params=_params(3),
        name="flash_prompt",
    )(*lams, qt, kb, vt, subln_col)


def _da_qkv_sample_body(x_ref, w_ref, c_ref, up_ref, dn_ref, q_ref, k_ref, v_ref):
    d = x_ref.shape[1]
    p = _mm(x_ref[...].astype(BF16), w_ref[...])
    c, up, dn = c_ref[...], up_ref[...], dn_ref[...]
    q_ref[...] = _rope_rows(p[:, :d], c, up, dn) * (DA_HD ** -0.5)
    k_ref[...] = _rope_rows(p[:, d:2 * d], c, up, dn)
    v_ref[...] = p[:, 2 * d:]


def _da_qkv_sample(x, w_in, tables):
    n, d = x.shape
    c, up, dn = tables[:3]
    return pl.pallas_call(
        _da_qkv_sample_body,
        out_shape=[jax.ShapeDtypeStruct((n, d), F32)] * 3,
        compiler_params=pltpu.CompilerParams(vmem_limit_bytes=VMEM_LIMIT),
        name="da_qkv_sample",
    )(x, w_in, c, up, dn)


def _lane_broadcast_columns(vec_bf, eye):
    w = vec_bf.shape[1]
    rows = jnp.broadcast_to(vec_bf, (LANES, w))
    return jnp.concatenate(
        [_mm_nt(eye, rows[:, i * LANES:(i + 1) * LANES]) for i in range(w // LANES)], axis=0)


def _identity_bf16():
    r = lax.broadcasted_iota(jnp.int32, (LANES, LANES), 0)
    c = lax.broadcasted_iota(jnp.int32, (LANES, LANES), 1)
    return jnp.where(r == c, 1.0, 0.0).astype(BF16)


def _decode_attn_body(npg, pt_ref, lq1_ref, lk1_ref, lq2_ref, lk2_ref, q_ref, kn_ref, vn_ref, *rest):
    del pt_ref
    kt_refs, v_refs = rest[:npg], rest[npg:2 * npg]
    sg_ref, o_ref, qb_ref, ex_ref, m_ref, l_ref, acc_ref = rest[2 * npg:]
    j = pl.program_id(1)
    n_comp = 2 * DA_HEADS
    flat = PAGE_SIZE * DA_HEADS

    @pl.when(j == 0)
    def _init():
        qb = _lane_broadcast_columns(q_ref[0].astype(BF16), _identity_bf16())
        qb_ref[...] = qb.reshape(qb_ref.shape)
        key = lax.broadcasted_iota(jnp.int32, (PAGE_SIZE, flat), 0)
        col = lax.broadcasted_iota(jnp.int32, (PAGE_SIZE, flat), 1)
        ex_ref[...] = jnp.where(col // DA_HEADS == key, 1.0, 0.0).astype(BF16)
        m_ref[...] = jnp.full(m_ref.shape, NEG_INF, F32)
        l_ref[...] = jnp.zeros(l_ref.shape, F32)
        acc_ref[...] = jnp.zeros(acc_ref.shape, F32)

    def scores(kt_of_comp):
        parts = []
        for comp in range(2):
            qb = qb_ref[pl.ds(comp, DA_HEADS, stride=2)]
            parts.append(jnp.sum(qb * kt_of_comp(comp), axis=1))
        return jnp.concatenate(parts, axis=0)

    def online_update(s, pv_of_p):
        m_prev = m_ref[...]
        m_new = jnp.maximum(m_prev, jnp.max(s, axis=-1, keepdims=True))
        alpha = jnp.exp(m_prev - m_new)
        p = jnp.exp(s - m_new)
        l_ref[...] = alpha * l_ref[...] + jnp.sum(p, axis=-1, keepdims=True)
        acc_ref[...] = alpha * acc_ref[...] + pv_of_p(p)
        m_ref[...] = m_new

    def pages_pv(p):
        row = lax.broadcasted_iota(jnp.int32, (n_comp, flat), 0)
        col = lax.broadcasted_iota(jnp.int32, (n_comp, flat), 1)
        own_head = col % DA_HEADS == row % DA_HEADS
        out = jnp.zeros((n_comp, LANES), F32)
        for g in range(npg):
            pg = p[:, g * PAGE_SIZE:(g + 1) * PAGE_SIZE].astype(BF16)
            spread = jnp.where(own_head, _mm(pg, ex_ref[...]), 0.0).astype(BF16)
            out = out + _mm(spread, v_refs[g][0].reshape(flat, LANES).astype(BF16))
        return out

    s_pages = jnp.concatenate(
        [scores(lambda comp, r=r: r[0, pl.ds(comp, DA_HEADS, stride=2)]) for r in kt_refs], axis=1)
    online_update(s_pages, pages_pv)

    @pl.when(j == pl.num_programs(1) - 1)
    def _finish():
        kb = _lane_broadcast_columns(kn_ref[0].astype(BF16), _identity_bf16())
        kb = kb.reshape(DA_HEADS, 2, DA_HD, LANES)
        s_self = scores(lambda comp: kb[:, comp])
        lane = lax.broadcasted_iota(jnp.int32, (n_comp, LANES), 1)
        s_self = jnp.where(lane == 0, s_self, NEG_INF)
        vn = vn_ref[0]
        v_rows = jnp.concatenate(
            [vn[:, h * LANES:(h + 1) * LANES] for h in range(DA_HEADS)] * 2, axis=0)
        online_update(s_self, lambda p: jnp.sum(p, axis=-1, keepdims=True) * v_rows)
        lam = _diff_lambda(lq1_ref, lk1_ref, lq2_ref, lk2_ref)
        o = acc_ref[...] / l_ref[...]
        o = o[0:DA_HEADS, :] - lam * o[DA_HEADS:, :]
        o_ref[0] = _rms_norm(o, sg_ref[...]) * (1.0 - DA_LAMBDA_INIT)


def _decode_attn(q, k_new, v_new, cache_kt, cache_v, page_table, lams, subln_g):
    n, d = q.shape
    n_pages = page_table.shape[1]
    npg = math.gcd(DECODE_PAGES_PER_STEP, n_pages)
    n_comp = 2 * DA_HEADS
    hw = 2 * DA_HD
    row3 = lambda a: a.reshape(n, 1, d)
    vec_spec = pl.BlockSpec((1, 1, d), lambda s, j, pt: (s, 0, 0))
    small = lambda shape: pl.BlockSpec(shape, lambda s, j, pt: (0,) * len(shape))
    page = lambda g: (lambda s, j, pt: (pt[s * n_pages + j * npg + g], 0, 0, 0))
    grid_spec = pltpu.PrefetchScalarGridSpec(
        num_scalar_prefetch=1,
        grid=(n, n_pages // npg),
        in_specs=[small((1, DA_HD))] * 4 + [vec_spec, vec_spec, vec_spec]
        + [pl.BlockSpec((1, n_comp, DA_HD, PAGE_SIZE), page(g)) for g in range(npg)]
        + [pl.BlockSpec((1, PAGE_SIZE, DA_HEADS, hw), page(g)) for g in range(npg)]
        + [small((1, hw))],
        out_specs=pl.BlockSpec((1, DA_HEADS, hw), lambda s, j, pt: (s, 0, 0)),
        scratch_shapes=[
            pltpu.VMEM((n_comp, DA_HD, LANES), F32),
            pltpu.VMEM((PAGE_SIZE, PAGE_SIZE * DA_HEADS), BF16),
            pltpu.VMEM((n_comp, 1), F32),
            pltpu.VMEM((n_comp, 1), F32),
            pltpu.VMEM((n_comp, hw), F32),
        ],
    )
    return pl.pallas_call(
        functools.partial(_decode_attn_body, npg),
        grid_spec=grid_spec,
        out_shape=jax.ShapeDtypeStruct((n, DA_HEADS, hw), F32),
        compiler_params=_params(2),
        name="decode_attn",
    )(page_table.reshape(-1), *lams, row3(q), row3(k_new), row3(v_new),
      *([cache_kt] * npg), *([cache_v] * npg), subln_g)


def _cumsum_rows(g, tri):
    hi = g.astype(BF16)
    r1 = g - hi.astype(F32)
    mid = r1.astype(BF16)
    lo = (r1 - mid.astype(F32)).astype(BF16)
    return _mm(tri, hi) + _mm(tri, mid) + _mm(tri, lo)


def _chunk_recurrence(q, k, v, g, st_ref, n_heads, dk, dv, post):
    c = q.shape[0]
    row = lax.broadcasted_iota(jnp.int32, (c, c), 0)
    col = lax.broadcasted_iota(jnp.int32, (c, c), 1)
    causal = row >= col
    big_g = _cumsum_rows(g, jnp.where(causal, 1.0, 0.0).astype(BF16))
    g_mid = big_g[c // 2 - 1:c // 2, :]
    g_last = big_g[c - 1:c, :]
    q_state = (q * jnp.exp(big_g)).astype(BF16)
    q_dec = (q * jnp.exp(big_g - g_mid)).astype(BF16)
    k_dec = (k * jnp.exp(g_mid - big_g)).astype(BF16)
    k_tail = (k * jnp.exp(g_last - big_g)).astype(BF16)
    decay = jnp.exp(g_last)
    vb = v.astype(BF16)
    outs = []
    for h in range(n_heads):
        ks = slice(h * dk, (h + 1) * dk)
        vs = slice(h * dv, (h + 1) * dv)
        scores = jnp.where(causal, _mm_nt(q_dec[:, ks], k_dec[:, ks]), 0.0)
        state = st_ref[h]
        o = _mm_nt(q_state[:, ks], state.astype(BF16)) + _mm(scores.astype(BF16), vb[:, vs])
        st_ref[h] = state * decay[:, ks] + _mm_tn(vb[:, vs], k_tail[:, ks])
        outs.append(post(o, h))
    return jnp.concatenate(outs, axis=1)


def _hgrn_lower_bound(lbl_ref, layer):
    logits = lbl_ref[...]
    e = jnp.exp(logits - jnp.max(logits, axis=0, keepdims=True))
    p = e / jnp.sum(e, axis=0, keepdims=True)
    return jnp.sum(p[0:layer + 1, :], axis=0, keepdims=True) - p[0:1, :]


def _hgrn_gates(p, lb, d):
    forget = lb + (1.0 - lb) * jax.nn.sigmoid(p[:, d:2 * d])
    return _silu(p[:, :d]), 1.0 - forget, p[:, 2 * d:], jnp.log(forget)


def _gla_gates(p, gk, wgk2_ref, bgk2_ref, d):
    key = d // 2
    z = _mm(gk.astype(BF16), wgk2_ref[...]) + bgk2_ref[...]
    log_sig = jnp.minimum(z, 0.0) - jnp.log(1.0 + jnp.exp(-jnp.abs(z)))
    q = p[:, :key] * ((key // GLA_HEADS) ** -0.5)
    return q, p[:, key:2 * key], p[:, 2 * key:2 * key + d], log_sig / GLA_GATE_NORM


def _recurrent_prompt_body(kind, layer, *refs):
    if kind == "hgrn":
        (x_ref, win_ref, lbl_ref, ng_ref, wout_ref, g_ref, b_ref,
         o_ref, sfin_ref, st_ref) = refs
    else:
        (x_ref, win_ref, wgk_ref, wgk2_ref, bgk2_ref, ng_ref, wout_ref, g_ref, b_ref,
         o_ref, sfin_ref, st_ref) = refs
    t = pl.program_id(1)
    tm, d = x_ref.shape
    n_heads, dv, dk = st_ref.shape

    @pl.when(t == 0)
    def _reset():
        st_ref[...] = jnp.zeros(st_ref.shape, F32)

    x = x_ref[...]
    xb = x.astype(BF16)
    p = _mm(xb, win_ref[...])
    ng = ng_ref[...]
    if kind == "hgrn":
        q, k, v, g = _hgrn_gates(p, _hgrn_lower_bound(lbl_ref, layer), d)
        post = lambda o, h: _rms_norm(o, ng)
    else:
        q, k, v, g = _gla_gates(p, _mm(xb, wgk_ref[...]), wgk2_ref, bgk2_ref, d)
        gate = p[:, 2 * d:3 * d]
        post = None
    chunk = min(REC_CHUNK, tm)
    outs = []
    for ci in range(tm // chunk):
        rows = slice(ci * chunk, (ci + 1) * chunk)
        if kind == "gla":
            gate_c = gate[rows]
            post = lambda o, h, gate_c=gate_c: _rms_norm(o, ng) * _silu(gate_c[:, h * dv:(h + 1) * dv])
        outs.append(_chunk_recurrence(q[rows], k[rows], v[rows], g[rows], st_ref,
                                      n_heads, dk, dv, post))
    on = jnp.concatenate(outs, axis=0)
    y = _mm(on.astype(BF16), wout_ref[...])
    o_ref[...] = _layer_norm(ALPHA * x + y, g_ref[...], b_ref[...])

    @pl.when(t == pl.num_programs(1) - 1)
    def _state():
        for h in range(n_heads):
            sfin_ref[0, h] = st_ref[h].T


def _recurrent_prompt(kind, layer, x, n_seq, n_heads, dk, dv, weights, w_out, g, b):
    t, d = x.shape
    seq = t // n_seq
    tm = min(MIX_TILE, seq)
    nt = seq // tm
    row = lambda s, i: (s * nt + i, 0)
    w_specs = [_const_spec(w.shape) for w in weights]
    return pl.pallas_call(
        functools.partial(_recurrent_prompt_body, kind, layer),
        grid=(n_seq, nt),
        in_specs=[pl.BlockSpec((tm, d), row)] + w_specs + [
            _const_spec((d, d)), _const_spec((1, d)), _const_spec((1, d))],
        out_specs=[
            pl.BlockSpec((tm, d), row),
            pl.BlockSpec((1, n_heads, dk, dv), lambda s, i: (s, 0, 0, 0)),
        ],
        out_shape=[
            jax.ShapeDtypeStruct((t, d), F32),
            jax.ShapeDtypeStruct((n_seq, n_heads, dk, dv), F32),
        ],
        scratch_shapes=[pltpu.VMEM((n_heads, dv, dk), F32)],
        compiler_params=_params(2),
        name=kind + "_prompt",
    )(x, *weights, w_out, g, b)


def _recurrent_sample_body(kind, layer, *refs):
    if kind == "hgrn":
        (x_ref, s_ref, win_ref, lbl_ref, ng_ref, wout_ref, g_ref, b_ref,
         o_ref, snew_ref, q_scr, k_scr, v_scr, f_scr, gate_scr, on_scr) = refs
    else:
        (x_ref, s_ref, win_ref, wgk_ref, wgk2_ref, bgk2_ref, ng_ref, wout_ref, g_ref, b_ref,
         o_ref, snew_ref, q_scr, k_scr, v_scr, f_scr, gate_scr, on_scr) = refs
    n = pl.program_id(0)
    d = x_ref.shape[1]
    _, n_heads, dk, dv = s_ref.shape

    @pl.when(n == 0)
    def _project():
        xb = x_ref[...].astype(BF16)
        p = _mm(xb, win_ref[...])
        if kind == "hgrn":
            q, k, v, g = _hgrn_gates(p, _hgrn_lower_bound(lbl_ref, layer), d)
            gate_scr[...] = jnp.zeros(gate_scr.shape, F32)
        else:
            q, k, v, g = _gla_gates(p, _mm(xb, wgk_ref[...]), wgk2_ref, bgk2_ref, d)
            gate_scr[...] = p[:, 2 * d:3 * d]
        q_scr[...] = q
        k_scr[...] = k
        v_scr[...] = v
        f_scr[...] = jnp.exp(g)

    eye = _identity_bf16()
    q = q_scr[pl.ds(n, 1), :]
    k = k_scr[pl.ds(n, 1), :]
    v = v_scr[pl.ds(n, 1), :]
    f = f_scr[pl.ds(n, 1), :]
    f_hi = f.astype(BF16)
    f_r = f - f_hi.astype(F32)
    f_mid = f_r.astype(BF16)
    f_lo = (f_r - f_mid.astype(F32)).astype(BF16)
    f_col = (_lane_broadcast_columns(f_hi, eye) + _lane_broadcast_columns(f_mid, eye)
             + _lane_broadcast_columns(f_lo, eye))
    k_col = _lane_broadcast_columns(k.astype(BF16), eye)
    reps = dv // LANES
    widen = lambda a: a if reps == 1 else jnp.concatenate([a] * reps, axis=1)
    outs = []
    for h in range(n_heads):
        ks = slice(h * dk, (h + 1) * dk)
        vs = slice(h * dv, (h + 1) * dv)
        s_new = widen(f_col[ks]) * s_ref[0, h] + widen(k_col[ks]) * v[:, vs]
        snew_ref[0, h] = s_new
        q_rows = jnp.broadcast_to(q[:, ks], (2 * SUBLANES, dk)).astype(BF16)
        outs.append(_mm(q_rows, s_new.astype(BF16))[0:1, :])
    on_scr[pl.ds(n, 1), :] = jnp.concatenate(outs, axis=1)

    @pl.when(n == pl.num_programs(0) - 1)
    def _finish():
        ng = ng_ref[...]
        o = on_scr[...]
        parts = []
        for h in range(n_heads):
            oh = _rms_norm(o[:, h * dv:(h + 1) * dv], ng)
            if kind == "gla":
                oh = oh * _silu(gate_scr[:, h * dv:(h + 1) * dv])
            parts.append(oh)
        y = _mm(jnp.concatenate(parts, axis=1).astype(BF16), wout_ref[...])
        o_ref[...] = _layer_norm(ALPHA * x_ref[...] + y, g_ref[...], b_ref[...])


def _recurrent_sample(kind, layer, x, state, weights, w_out, g, b):
    n, d = x.shape
    _, n_heads, dk, dv = state.shape
    w_specs = [_const_spec(w.shape) for w in weights]
    st_spec = pl.BlockSpec((1, n_heads, dk, dv), lambda s: (s, 0, 0, 0))
    return pl.pallas_call(
        functools.partial(_recurrent_sample_body, kind, layer),
        grid=(n,),
        in_specs=[_const_spec((n, d)), st_spec] + w_specs + [
            _const_spec((d, d)), _const_spec((1, d)), _const_spec((1, d))],
        out_specs=[_const_spec((n, d)), st_spec],
        out_shape=[jax.ShapeDtypeStruct((n, d), F32), jax.ShapeDtypeStruct(state.shape, F32)],
        scratch_shapes=[
            pltpu.VMEM((n, n_heads * dk), F32),
            pltpu.VMEM((n, n_heads * dk), F32),
            pltpu.VMEM((n, n_heads * dv), F32),
            pltpu.VMEM((n, n_heads * dk), F32),
            pltpu.VMEM((n, d), F32),
            pltpu.VMEM((n, n_heads * dv), F32),
        ],
        compiler_params=_params(1),
        name=kind + "_sample",
    )(x, state, *weights, w_out, g, b)


def kernel(x_prompt, x_sample, state_conv, cache_k, cache_v, page_table, state_hgrn, state_gla,
           ffn_w_up, ffn_w_down, ln_g, ln_b,
           conv_w_in, conv_w, conv_w_out,
           da_w_in, da_lambda_q1, da_lambda_k1, da_lambda_q2, da_lambda_k2, da_subln_g, da_w_out,
           hg_w_in, hg_lb_logits, hg_norm_g, hg_w_out,
           gla_w_in, gla_w_gk2, gla_b_gk2, gla_norm_g, gla_w_out):
    n_seq, seq, d = x_prompt.shape
    n_smp = x_sample.shape[0]
    xp = x_prompt.reshape(n_seq * seq, d)
    xs = x_sample.reshape(n_smp, d)
    bf = lambda w: w.astype(BF16)
    row = lambda v: v.reshape(1, -1)
    w_up, w_down = bf(ffn_w_up), bf(ffn_w_down)

    def ffn_pair(i, half, xp, xs):
        g, b = row(ln_g[i, half * 2]), row(ln_b[i, half * 2])
        return (_ffn(xp, w_up, w_down, i, half, g, b), _ffn(xs, w_up, w_down, i, half, g, b))

    xp, xs = ffn_pair(0, 0, xp, xs)
    g, b = row(ln_g[0, 1]), row(ln_b[0, 1])
    cw_in, cw_out = bf(conv_w_in), bf(conv_w_out)
    xp, conv_p = _conv_prompt(xp, n_seq, cw_in, conv_w, cw_out, g, b)
    xs, u_s = _conv_sample(xs, state_conv[:, 0], state_conv[:, 1], cw_in, conv_w, cw_out, g, b)
    conv_s = jnp.stack([state_conv[:, 1], u_s], axis=1)
    xp, xs = ffn_pair(0, 1, xp, xs)

    xp, xs = ffn_pair(1, 0, xp, xs)
    g, b = row(ln_g[1, 1]), row(ln_b[1, 1])
    lams = [row(v) for v in (da_lambda_q1, da_lambda_k1, da_lambda_q2, da_lambda_k2)]
    subln = row(da_subln_g)
    dw_in, dw_out = bf(da_w_in), bf(da_w_out)
    past = page_table.shape[1] * PAGE_SIZE
    qt, kt, kb, v, vt = _da_qkv_prompt(
        xp, n_seq, dw_in[:, :d].T, dw_in[:, d:2 * d].T, dw_in[:, 2 * d:],
        _rope_tables(jnp.arange(seq, dtype=jnp.int32)))
    on = _flash_prompt(qt, kb, vt, lams, da_subln_g.reshape(-1, 1))
    k_p = kt.reshape(n_seq, 2 * DA_HEADS, DA_HD, seq).transpose(0, 3, 1, 2)
    v_p = v.reshape(n_seq, seq, DA_HEADS, 2 * DA_HD)
    xp = _proj_norm(on, xp, dw_out, g, b)
    qs, ks, vs = _da_qkv_sample(xs, dw_in, _rope_tables(jnp.full((1,), past, jnp.int32)))
    ons = _decode_attn(qs, ks, vs, cache_k.transpose(0, 2, 3, 1), cache_v, page_table, lams, subln)
    k_s = ks.reshape(n_smp, 1, 2 * DA_HEADS, DA_HD)
    v_s = vs.reshape(n_smp, 1, DA_HEADS, 2 * DA_HD)
    xs = _proj_norm(ons.reshape(n_smp, d).astype(BF16), xs, dw_out, g, b)
    xp, xs = ffn_pair(1, 1, xp, xs)

    xp, xs = ffn_pair(2, 0, xp, xs)
    g, b = row(ln_g[2, 1]), row(ln_b[2, 1])
    hg_weights = (bf(hg_w_in), hg_lb_logits, row(hg_norm_g))
    dk = d // HG_HEADS
    xp, hg_p = _recurrent_prompt("hgrn", 2, xp, n_seq, HG_HEADS, dk, dk, hg_weights,
                                 bf(hg_w_out), g, b)
    xs, hg_s = _recurrent_sample("hgrn", 2, xs, state_hgrn, hg_weights, bf(hg_w_out), g, b)
    xp, xs = ffn_pair(2, 1, xp, xs)

    xp, xs = ffn_pair(3, 0, xp, xs)
    g, b = row(ln_g[3, 1]), row(ln_b[3, 1])
    main = 3 * d
    w_gk = jnp.pad(bf(gla_w_in[:, main:]), ((0, 0), (0, LANES - GLA_LOWRANK)))
    w_gk2 = jnp.pad(bf(gla_w_gk2), ((0, LANES - GLA_LOWRANK), (0, 0)))
    gla_weights = (bf(gla_w_in[:, :main]), w_gk, w_gk2, row(gla_b_gk2), row(gla_norm_g))
    gdk = d // 2 // GLA_HEADS
    gdv = d // GLA_HEADS
    xp, gla_p = _recurrent_prompt("gla", 3, xp, n_seq, GLA_HEADS, gdk, gdv, gla_weights,
                                  bf(gla_w_out), g, b)
    xs, gla_s = _recurrent_sample("gla", 3, xs, state_gla, gla_weights, bf(gla_w_out), g, b)
    xp, xs = ffn_pair(3, 1, xp, xs)

    return (xp.reshape(n_seq, seq, d), xs.reshape(n_smp, 1, d), conv_p, conv_s,
            k_p, v_p, k_s, v_s, hg_p, hg_s, gla_p, gla_s)
```

```python
import functools
import math

import jax
import jax.numpy as jnp
from jax import lax
from jax.experimental import pallas as pl
from jax.experimental.pallas import tpu as pltpu

F32 = jnp.float32
BF16 = jnp.bfloat16

DEPTH = 4
LN_EPS = 1e-5
RMS_EPS = 1e-6
ALPHA = (2.0 * DEPTH) ** 0.25
CONV_W = 3
DA_HEADS = 8
DA_HD = 64
ROT_DIM = DA_HD // 4
ROPE_THETA = 500000.0
DA_LAMBDA_INIT = 0.8 - 0.6 * math.exp(-0.3 * 1)
NEG_INF = -1e30
PAGE_SIZE = 128
HG_HEADS = 8
GLA_HEADS = 4
GLA_LOWRANK = 16
GLA_GATE_NORM = 16.0

LANES = 128
SUBLANES = 8
MXU_DIM = 256
VMEM_LIMIT = 56 * 1024 * 1024

TOKEN_TILE = 512
FFN_CHUNK = 1408
MIX_TILE = 256
REC_CHUNK = 64
ATT_TILE = 512
DECODE_PAGES_PER_STEP = 8


def _mm(a, b):
    return jnp.dot(a, b, preferred_element_type=F32)


def _mm_nt(a, b):
    return lax.dot_general(a, b, (((1,), (1,)), ((), ())), preferred_element_type=F32)


def _mm_tn(a, b):
    return lax.dot_general(a, b, (((0,), (0,)), ((), ())), preferred_element_type=F32)


def _layer_norm(z, g, b):
    mu = jnp.mean(z, axis=-1, keepdims=True)
    zc = z - mu
    var = jnp.mean(zc * zc, axis=-1, keepdims=True)
    return zc * lax.rsqrt(var + LN_EPS) * g + b


def _rms_norm(o, g):
    return o * lax.rsqrt(jnp.mean(o * o, axis=-1, keepdims=True) + RMS_EPS) * g


def _silu(a):
    return a * jax.nn.sigmoid(a)


def _params(n_axes):
    return pltpu.CompilerParams(dimension_semantics=("arbitrary",) * n_axes,
                                vmem_limit_bytes=VMEM_LIMIT)


def _const_spec(shape):
    nd = len(shape)
    return pl.BlockSpec(shape, lambda *_: (0,) * nd)


def _ffn_chunks(f):
    n = max(1, f // FFN_CHUNK)
    edges = [MXU_DIM * round(i * f / n / MXU_DIM) for i in range(n)] + [f]
    return list(zip(edges[:-1], edges[1:]))


def _ffn_body(nt, x_ref, xs_ref, wu_ref, wd_ref, g_ref, b_ref, o_ref, os_ref, z_ref):
    i = pl.program_id(0)
    f = wd_ref.shape[0]
    g = g_ref[...]
    b = b_ref[...]

    def residual_plus_ffn(x, norm_of=None):
        xb = x.astype(BF16)
        acc = None
        normed = None
        for c0, c1 in _ffn_chunks(f):
            a = _mm(xb, wu_ref[:, c0:c1])
            u = _mm(xb, wu_ref[:, f + c0:f + c1])
            if norm_of is not None and normed is None:
                normed = _layer_norm(norm_of, g, b)
            h = (_silu(a) * u).astype(BF16)
            part = _mm(h, wd_ref[c0:c1, :])
            acc = part if acc is None else acc + part
        return ALPHA * x + 0.5 * acc, normed

    @pl.when(i == 0)
    def _init():
        z_ref[...] = jnp.zeros(z_ref.shape, F32)

    @pl.when(i < nt)
    def _tile():
        z_new, normed = residual_plus_ffn(x_ref[...], z_ref[...])
        o_ref[...] = normed
        z_ref[...] = z_new

    @pl.when(i == nt)
    def _tail():
        o_ref[...] = _layer_norm(z_ref[...], g, b)
        zs, _ = residual_plus_ffn(xs_ref[...])
        os_ref[...] = _layer_norm(zs, g, b)


def _ffn(x, xs, w_up, w_down, layer, half, g, b):
    t, d = x.shape
    ns = xs.shape[0]
    f = w_down.shape[2]
    tm = min(TOKEN_TILE, t)
    nt = t // tm
    once = pl.Buffered(1)
    return pl.pallas_call(
        functools.partial(_ffn_body, nt),
        grid=(nt + 1,),
        in_specs=[
            pl.BlockSpec((tm, d), lambda i: (jnp.minimum(i, nt - 1), 0)),
            _const_spec((ns, d)),
            pl.BlockSpec((None, None, d, 2 * f), lambda i: (layer, half, 0, 0), pipeline_mode=once),
            pl.BlockSpec((None, None, f, d), lambda i: (layer, half, 0, 0), pipeline_mode=once),
            _const_spec((1, d)),
            _const_spec((1, d)),
        ],
        out_specs=[
            pl.BlockSpec((tm, d), lambda i: (jnp.maximum(i - 1, 0), 0)),
            _const_spec((ns, d)),
        ],
        out_shape=[jax.ShapeDtypeStruct((t, d), F32), jax.ShapeDtypeStruct((ns, d), F32)],
        scratch_shapes=[pltpu.VMEM((tm, d), F32)],
        compiler_params=_params(1),
        name="ffn",
    )(x, xs, w_up, w_down, g, b)


def _proj_norm_body(y_ref, x_ref, w_ref, g_ref, b_ref, o_ref):
    y = _mm(y_ref[...], w_ref[...])
    o_ref[...] = _layer_norm(ALPHA * x_ref[...] + y, g_ref[...], b_ref[...])


def _proj_norm(y, x, w, g, b):
    t, d = x.shape
    tm = min(TOKEN_TILE, t)
    return pl.pallas_call(
        _proj_norm_body,
        grid=(t // tm,),
        in_specs=[
            pl.BlockSpec((tm, d), lambda i: (i, 0)),
            pl.BlockSpec((tm, d), lambda i: (i, 0)),
            _const_spec((d, d)),
            _const_spec((1, d)),
            _const_spec((1, d)),
        ],
        out_specs=pl.BlockSpec((tm, d), lambda i: (i, 0)),
        out_shape=jax.ShapeDtypeStruct((t, d), F32),
        compiler_params=_params(1),
        name="proj_norm",
    )(y, x, w, g, b)


def _conv_prompt_body(x_ref, win_ref, cw_ref, wout_ref, g_ref, b_ref, o_ref, st_ref, ubuf_ref):
    t = pl.program_id(1)
    tm, d = x_ref.shape

    @pl.when(t == 0)
    def _reset():
        ubuf_ref[0:SUBLANES, :] = jnp.zeros((SUBLANES, d), F32)

    x = x_ref[...]
    p = _mm(x.astype(BF16), win_ref[...])
    u = p[:, d:2 * d] * p[:, 2 * d:]
    ubuf_ref[SUBLANES:, :] = u
    ue = ubuf_ref[...]
    um1 = pltpu.roll(ue, 1, 0)[SUBLANES:, :]
    um2 = pltpu.roll(ue, 2, 0)[SUBLANES:, :]
    cw = cw_ref[...]
    conv = cw[0:1, :] * um2 + cw[1:2, :] * um1 + cw[2:3, :] * u
    y = _mm((p[:, :d] * conv).astype(BF16), wout_ref[...])
    o_ref[...] = _layer_norm(ALPHA * x + y, g_ref[...], b_ref[...])
    ubuf_ref[0:SUBLANES, :] = ubuf_ref[tm:tm + SUBLANES, :]

    @pl.when(t == pl.num_programs(1) - 1)
    def _state():
        st_ref[0] = ubuf_ref[tm + SUBLANES - (CONV_W - 1):tm + SUBLANES, :]


def _conv_prompt(x, n_seq, w_in, conv_w, w_out, g, b):
    t, d = x.shape
    seq = t // n_seq
    tm = min(TOKEN_TILE, seq)
    nt = seq // tm
    return pl.pallas_call(
        _conv_prompt_body,
        grid=(n_seq, nt),
        in_specs=[
            pl.BlockSpec((tm, d), lambda s, i: (s * nt + i, 0)),
            _const_spec((d, 3 * d)),
            _const_spec((CONV_W, d)),
            _const_spec((d, d)),
            _const_spec((1, d)),
            _const_spec((1, d)),
        ],
        out_specs=[
            pl.BlockSpec((tm, d), lambda s, i: (s * nt + i, 0)),
            pl.BlockSpec((1, CONV_W - 1, d), lambda s, i: (s, 0, 0)),
        ],
        out_shape=[
            jax.ShapeDtypeStruct((t, d), F32),
            jax.ShapeDtypeStruct((n_seq, CONV_W - 1, d), F32),
        ],
        scratch_shapes=[pltpu.VMEM((tm + SUBLANES, d), F32)],
        compiler_params=_params(2),
        name="conv_prompt",
    )(x, w_in, conv_w, w_out, g, b)


def _conv_sample_body(x_ref, s0_ref, s1_ref, win_ref, cw_ref, wout_ref, g_ref, b_ref,
                      o_ref, u_ref):
    d = x_ref.shape[1]
    x = x_ref[...]
    p = _mm(x.astype(BF16), win_ref[...])
    u = p[:, d:2 * d] * p[:, 2 * d:]
    cw = cw_ref[...]
    conv = cw[0:1, :] * s0_ref[...] + cw[1:2, :] * s1_ref[...] + cw[2:3, :] * u
    y = _mm((p[:, :d] * conv).astype(BF16), wout_ref[...])
    o_ref[...] = _layer_norm(ALPHA * x + y, g_ref[...], b_ref[...])
    u_ref[...] = u


def _conv_sample(x, s0, s1, w_in, conv_w, w_out, g, b):
    n, d = x.shape
    return pl.pallas_call(
        _conv_sample_body,
        out_shape=[jax.ShapeDtypeStruct((n, d), F32), jax.ShapeDtypeStruct((n, d), F32)],
        compiler_params=pltpu.CompilerParams(vmem_limit_bytes=VMEM_LIMIT),
        name="conv_sample",
    )(x, s0, s1, w_in, conv_w, w_out, g, b)


def _rope_tables(pos):
    inv_freq = 1.0 / (ROPE_THETA ** (jnp.arange(0, ROT_DIM, 2, dtype=F32) / ROT_DIM))
    ang = pos.astype(F32)[:, None] * inv_freq[None, :]
    cos, sin = jnp.cos(ang), jnp.sin(ang)
    half = ROT_DIM // 2
    n = pos.shape[0]
    rest = DA_HD - ROT_DIM
    c64 = jnp.concatenate([cos, cos, jnp.ones((n, rest), F32)], axis=1)
    up64 = jnp.concatenate([-sin, jnp.zeros((n, DA_HD - half), F32)], axis=1)
    dn64 = jnp.concatenate([jnp.zeros((n, half), F32), sin, jnp.zeros((n, rest), F32)], axis=1)
    rep = LANES // DA_HD
    return (jnp.tile(c64, (1, rep)), jnp.tile(up64, (1, rep)), jnp.tile(dn64, (1, rep)),
            cos.T, sin.T)


def _rope_rows(x, c, s_up, s_dn):
    half = ROT_DIM // 2
    blocks = []
    for i in range(x.shape[1] // LANES):
        xb = x[:, i * LANES:(i + 1) * LANES]
        blocks.append(xb * c + pltpu.roll(xb, LANES - half, 1) * s_up + pltpu.roll(xb, half, 1) * s_dn)
    return jnp.concatenate(blocks, axis=1)


def _rope_cols(xt, ct, st):
    d, cols = xt.shape
    half = ROT_DIM // 2
    x3 = xt.reshape(d // DA_HD, DA_HD, cols)
    x1 = x3[:, 0:half, :]
    x2 = x3[:, half:ROT_DIM, :]
    x3 = jnp.concatenate([x1 * ct - x2 * st, x2 * ct + x1 * st, x3[:, ROT_DIM:, :]], axis=1)
    return x3.reshape(d, cols)


def _da_qkv_prompt_body(x_ref, wqt_ref, wkt_ref, wv_ref, ct_ref, st_ref,
                        qt_ref, kt_ref, kb_ref, v_ref, vt_ref):
    xb = x_ref[...].astype(BF16)
    ct = ct_ref[...]
    st = st_ref[...]
    qt = _rope_cols(_mm_nt(wqt_ref[...], xb), ct, st)
    qt_ref[0] = (qt * (DA_HD ** -0.5)).astype(BF16)
    kt = _rope_cols(_mm_nt(wkt_ref[...], xb), ct, st)
    kt_ref[0] = kt
    kb_ref[...] = kt.T.astype(BF16)
    v = _mm(xb, wv_ref[...])
    v_ref[...] = v
    vt_ref[0, 0] = v.T.astype(BF16)


def _da_qkv_prompt(x, n_seq, wqt, wkt, wv, tables):
    t, d = x.shape
    seq = t // n_seq
    tm = min(ATT_TILE, seq)
    nt = seq // tm
    ct, st = tables[3:]
    half = ROT_DIM // 2
    row = lambda s, i: (s * nt + i, 0)
    return pl.pallas_call(
        _da_qkv_prompt_body,
        grid=(n_seq, nt),
        in_specs=[
            pl.BlockSpec((tm, d), row),
            _const_spec((d, d)),
            _const_spec((d, d)),
            _const_spec((d, d)),
            pl.BlockSpec((half, tm), lambda s, i: (0, i)),
            pl.BlockSpec((half, tm), lambda s, i: (0, i)),
        ],
        out_specs=[
            pl.BlockSpec((1, d, tm), lambda s, i: (s, 0, i)),
            pl.BlockSpec((1, d, tm), lambda s, i: (s, 0, i)),
            pl.BlockSpec((tm, d), row),
            pl.BlockSpec((tm, d), row),
            pl.BlockSpec((1, 1, d, tm), lambda s, i: (s, i, 0, 0)),
        ],
        out_shape=[
            jax.ShapeDtypeStruct((n_seq, d, seq), BF16),
            jax.ShapeDtypeStruct((n_seq, d, seq), F32),
            jax.ShapeDtypeStruct((t, d), BF16),
            jax.ShapeDtypeStruct((t, d), F32),
            jax.ShapeDtypeStruct((n_seq, nt, d, tm), BF16),
        ],
        compiler_params=_params(2),
        name="da_qkv_prompt",
    )(x, wqt, wkt, wv, ct, st)


def _diff_lambda(lq1_ref, lk1_ref, lq2_ref, lk2_ref):
    s1 = jnp.sum(lq1_ref[...] * lk1_ref[...], axis=-1, keepdims=True)
    s2 = jnp.sum(lq2_ref[...] * lk2_ref[...], axis=-1, keepdims=True)
    return jnp.exp(s1) - jnp.exp(s2) + DA_LAMBDA_INIT


def _flash_body(lq1_ref, lk1_ref, lq2_ref, lk2_ref, qt_ref, k_ref, vt_ref, sg_ref, o_ref,
                q2_ref, m_ref, l_ref, acc_ref):
    qi = pl.program_id(2)
    tq = qt_ref.shape[2]
    tk = vt_ref.shape[3]
    n_col = 2 * tq
    cb = min(2 * LANES, tq)

    qt = qt_ref[0]
    sub = lax.broadcasted_iota(jnp.int32, qt.shape, 0)
    zero = jnp.zeros_like(qt)
    q2_ref[:, 0:tq] = jnp.where(sub < DA_HD, qt, zero)
    q2_ref[:, tq:] = jnp.where(sub >= DA_HD, qt, zero)
    m_ref[...] = jnp.full(m_ref.shape, NEG_INF, F32)
    l_ref[...] = jnp.zeros(l_ref.shape, F32)
    acc_ref[...] = jnp.zeros(acc_ref.shape, F32)

    starts = list(range(0, n_col, cb))

    def kv_tiles(kjs, diagonal):
        k_tiles = [k_ref[pl.ds(pl.multiple_of(kj * tk, tk), tk), :] for kj in kjs]
        vt_tiles = [vt_ref[0, kj] for kj in kjs]
        items = [(t, c0) for t in range(len(kjs)) for c0 in starts]
        n_items = len(items)

        def score(n):
            t, c0 = items[n]
            return _mm(k_tiles[t], q2_ref[:, c0:c0 + cb])

        def max_pass(n, st):
            c0 = items[n][1]
            if diagonal:
                key = lax.broadcasted_iota(jnp.int32, st.shape, 0)
                qry = lax.broadcasted_iota(jnp.int32, st.shape, 1) + (c0 % tq)
                st = jnp.where(key <= qry, st, NEG_INF)
            m_prev = m_ref[:, c0:c0 + cb]
            return st, m_prev, jnp.maximum(m_prev, jnp.max(st, axis=0, keepdims=True))

        def exp_pass(n, st, m_prev, m_new):
            t, c0 = items[n]
            cs = slice(c0, c0 + cb)
            alpha = jnp.exp(m_prev - m_new)
            pt = jnp.exp(st - m_new)
            l_ref[:, cs] = alpha * l_ref[:, cs] + jnp.sum(pt, axis=0, keepdims=True)
            acc_ref[:, cs] = alpha * acc_ref[:, cs] + _mm(vt_tiles[t], pt.astype(BF16))
            m_ref[:, cs] = m_new

        raw = {n: score(n) for n in range(min(2, n_items))}
        ready = {0: max_pass(0, raw.pop(0))}
        for n in range(n_items):
            if n + 2 < n_items:
                raw[n + 2] = score(n + 2)
            if n + 1 < n_items:
                ready[n + 1] = max_pass(n + 1, raw.pop(n + 1))
            exp_pass(n, *ready.pop(n))

    def tile_pair(jj, carry):
        kv_tiles([2 * jj, 2 * jj + 1], False)
        return carry

    lax.fori_loop(0, lax.shift_right_logical(qi, 1), tile_pair, 0)

    @pl.when(jnp.bitwise_and(qi, 1) == 1)
    def _odd_tile():
        kv_tiles([qi - 1], False)

    kv_tiles([qi], True)

    lam = _diff_lambda(lq1_ref, lk1_ref, lq2_ref, lk2_ref)
    o = acc_ref[...] / l_ref[...]
    o = o[:, 0:tq] - lam * o[:, tq:]
    r = lax.rsqrt(jnp.mean(o * o, axis=0, keepdims=True) + RMS_EPS)
    o = o * r * sg_ref[...] * (1.0 - DA_LAMBDA_INIT)
    o_ref[...] = o.T.astype(BF16)


def _flash_prompt(qt, kb, vt, lams, subln_col):
    n_seq, d, seq = qt.shape
    nq, ta = vt.shape[1], vt.shape[3]
    hw = 2 * DA_HD
    return pl.pallas_call(
        _flash_body,
        grid=(n_seq, DA_HEADS, nq),
        in_specs=[_const_spec((1, DA_HD))] * 4 + [
            pl.BlockSpec((1, hw, ta), lambda s, h, i: (s, h, i)),
            pl.BlockSpec((seq, hw), lambda s, h, i: (s, h)),
            pl.BlockSpec((1, nq, hw, ta), lambda s, h, i: (s, 0, h, 0)),
            _const_spec((hw, 1)),
        ],
        out_specs=pl.BlockSpec((ta, hw), lambda s, h, i: (s * nq + i, h)),
        out_shape=jax.ShapeDtypeStruct((n_seq * seq, d), BF16),
        scratch_shapes=[
            pltpu.VMEM((hw, 2 * ta), BF16),
            pltpu.VMEM((1, 2 * ta), F32),
            pltpu.VMEM((1, 2 * ta), F32),
            pltpu.VMEM((hw, 2 * ta), F32),
        ],
        compiler_params=_params(3),
        name="flash_prompt",
    )(*lams, qt, kb, vt, subln_col)


def _da_qkv_sample_body(x_ref, w_ref, c_ref, up_ref, dn_ref, q_ref, k_ref, v_ref):
    d = x_ref.shape[1]
    p = _mm(x_ref[...].astype(BF16), w_ref[...])
    c, up, dn = c_ref[...], up_ref[...], dn_ref[...]
    q_ref[...] = _rope_rows(p[:, :d], c, up, dn) * (DA_HD ** -0.5)
    k_ref[...] = _rope_rows(p[:, d:2 * d], c, up, dn)
    v_ref[...] = p[:, 2 * d:]


def _da_qkv_sample(x, w_in, tables):
    n, d = x.shape
    c, up, dn = tables[:3]
    return pl.pallas_call(
        _da_qkv_sample_body,
        out_shape=[jax.ShapeDtypeStruct((n, d), F32)] * 3,
        compiler_params=pltpu.CompilerParams(vmem_limit_bytes=VMEM_LIMIT),
        name="da_qkv_sample",
    )(x, w_in, c, up, dn)


def _lane_broadcast_columns(vec_bf, eye):
    w = vec_bf.shape[1]
    rows = jnp.broadcast_to(vec_bf, (LANES, w))
    return jnp.concatenate(
        [_mm_nt(eye, rows[:, i * LANES:(i + 1) * LANES]) for i in range(w // LANES)], axis=0)


def _identity_bf16():
    r = lax.broadcasted_iota(jnp.int32, (LANES, LANES), 0)
    c = lax.broadcasted_iota(jnp.int32, (LANES, LANES), 1)
    return jnp.where(r == c, 1.0, 0.0).astype(BF16)


def _decode_attn_body(npg, pt_ref, lq1_ref, lk1_ref, lq2_ref, lk2_ref, q_ref, kn_ref, vn_ref, *rest):
    del pt_ref
    kt_refs, v_refs = rest[:npg], rest[npg:2 * npg]
    sg_ref, o_ref, qb_ref, ex_ref, m_ref, l_ref, acc_ref = rest[2 * npg:]
    j = pl.program_id(1)
    n_comp = 2 * DA_HEADS
    flat = PAGE_SIZE * DA_HEADS

    @pl.when(j == 0)
    def _init():
        qb = _lane_broadcast_columns(q_ref[0].astype(BF16), _identity_bf16())
        qb_ref[...] = qb.reshape(qb_ref.shape)
        key = lax.broadcasted_iota(jnp.int32, (PAGE_SIZE, flat), 0)
        col = lax.broadcasted_iota(jnp.int32, (PAGE_SIZE, flat), 1)
        ex_ref[...] = jnp.where(col // DA_HEADS == key, 1.0, 0.0).astype(BF16)
        m_ref[...] = jnp.full(m_ref.shape, NEG_INF, F32)
        l_ref[...] = jnp.zeros(l_ref.shape, F32)
        acc_ref[...] = jnp.zeros(acc_ref.shape, F32)

    def scores(kt_of_comp):
        parts = []
        for comp in range(2):
            qb = qb_ref[pl.ds(comp, DA_HEADS, stride=2)]
            parts.append(jnp.sum(qb * kt_of_comp(comp), axis=1))
        return jnp.concatenate(parts, axis=0)

    def online_update(s, pv_of_p):
        m_prev = m_ref[...]
        m_new = jnp.maximum(m_prev, jnp.max(s, axis=-1, keepdims=True))
        alpha = jnp.exp(m_prev - m_new)
        p = jnp.exp(s - m_new)
        l_ref[...] = alpha * l_ref[...] + jnp.sum(p, axis=-1, keepdims=True)
        acc_ref[...] = alpha * acc_ref[...] + pv_of_p(p)
        m_ref[...] = m_new

    def pages_pv(p):
        row = lax.broadcasted_iota(jnp.int32, (n_comp, flat), 0)
        col = lax.broadcasted_iota(jnp.int32, (n_comp, flat), 1)
        own_head = col % DA_HEADS == row % DA_HEADS
        out = jnp.zeros((n_comp, LANES), F32)
        for g in range(npg):
            pg = p[:, g * PAGE_SIZE:(g + 1) * PAGE_SIZE].astype(BF16)
            spread = jnp.where(own_head, _mm(pg, ex_ref[...]), 0.0).astype(BF16)
            out = out + _mm(spread, v_refs[g][0].reshape(flat, LANES).astype(BF16))
        return out

    s_pages = jnp.concatenate(
        [scores(lambda comp, r=r: r[0, pl.ds(comp, DA_HEADS, stride=2)]) for r in kt_refs], axis=1)
    online_update(s_pages, pages_pv)

    @pl.when(j == pl.num_programs(1) - 1)
    def _finish():
        kb = _lane_broadcast_columns(kn_ref[0].astype(BF16), _identity_bf16())
        kb = kb.reshape(DA_HEADS, 2, DA_HD, LANES)
        s_self = scores(lambda comp: kb[:, comp])
        lane = lax.broadcasted_iota(jnp.int32, (n_comp, LANES), 1)
        s_self = jnp.where(lane == 0, s_self, NEG_INF)
        vn = vn_ref[0]
        v_rows = jnp.concatenate(
            [vn[:, h * LANES:(h + 1) * LANES] for h in range(DA_HEADS)] * 2, axis=0)
        online_update(s_self, lambda p: jnp.sum(p, axis=-1, keepdims=True) * v_rows)
        lam = _diff_lambda(lq1_ref, lk1_ref, lq2_ref, lk2_ref)
        o = acc_ref[...] / l_ref[...]
        o = o[0:DA_HEADS, :] - lam * o[DA_HEADS:, :]
        o_ref[0] = _rms_norm(o, sg_ref[...]) * (1.0 - DA_LAMBDA_INIT)


def _decode_attn(q, k_new, v_new, cache_kt, cache_v, page_table, lams, subln_g):
    n, d = q.shape
    n_pages = page_table.shape[1]
    npg = math.gcd(DECODE_PAGES_PER_STEP, n_pages)
    n_comp = 2 * DA_HEADS
    hw = 2 * DA_HD
    row3 = lambda a: a.reshape(n, 1, d)
    vec_spec = pl.BlockSpec((1, 1, d), lambda s, j, pt: (s, 0, 0))
    small = lambda shape: pl.BlockSpec(shape, lambda s, j, pt: (0,) * len(shape))
    page = lambda g: (lambda s, j, pt: (pt[s * n_pages + j * npg + g], 0, 0, 0))
    grid_spec = pltpu.PrefetchScalarGridSpec(
        num_scalar_prefetch=1,
        grid=(n, n_pages // npg),
        in_specs=[small((1, DA_HD))] * 4 + [vec_spec, vec_spec, vec_spec]
        + [pl.BlockSpec((1, n_comp, DA_HD, PAGE_SIZE), page(g)) for g in range(npg)]
        + [pl.BlockSpec((1, PAGE_SIZE, DA_HEADS, hw), page(g)) for g in range(npg)]
        + [small((1, hw))],
        out_specs=pl.BlockSpec((1, DA_HEADS, hw), lambda s, j, pt: (s, 0, 0)),
        scratch_shapes=[
            pltpu.VMEM((n_comp, DA_HD, LANES), F32),
            pltpu.VMEM((PAGE_SIZE, PAGE_SIZE * DA_HEADS), BF16),
            pltpu.VMEM((n_comp, 1), F32),
            pltpu.VMEM((n_comp, 1), F32),
            pltpu.VMEM((n_comp, hw), F32),
        ],
    )
    return pl.pallas_call(
        functools.partial(_decode_attn_body, npg),
        grid_spec=grid_spec,
        out_shape=jax.ShapeDtypeStruct((n, DA_HEADS, hw), F32),
        compiler_params=_params(2),
        name="decode_attn",
    )(page_table.reshape(-1), *lams, row3(q), row3(k_new), row3(v_new),
      *([cache_kt] * npg), *([cache_v] * npg), subln_g)


def _cumsum_rows(g, tri):
    hi = g.astype(BF16)
    r1 = g - hi.astype(F32)
    mid = r1.astype(BF16)
    lo = (r1 - mid.astype(F32)).astype(BF16)
    return _mm(tri, hi) + _mm(tri, mid) + _mm(tri, lo)


def _recurrence_tile(q, k, v, g, st_ref, n_heads, dk, dv, c, post):
    t = q.shape[0]
    nc = t // c
    row = lax.broadcasted_iota(jnp.int32, (t, t), 0)
    col = lax.broadcasted_iota(jnp.int32, (t, t), 1)
    same_chunk = row // c == col // c
    tri = jnp.where(same_chunk & (row >= col), 1.0, 0.0).astype(BF16)
    all_g = _cumsum_rows(g, tri)
    q_state, q_dec, k_dec, k_tail, decay, vb = [], [], [], [], [], []
    for ci in range(nc):
        rows = slice(ci * c, (ci + 1) * c)
        big_g = all_g[rows]
        g_mid = all_g[ci * c + c // 2 - 1:ci * c + c // 2, :]
        g_last = all_g[(ci + 1) * c - 1:(ci + 1) * c, :]
        q_state.append((q[rows] * jnp.exp(big_g)).astype(BF16))
        q_dec.append((q[rows] * jnp.exp(big_g - g_mid)).astype(BF16))
        k_dec.append((k[rows] * jnp.exp(g_mid - big_g)).astype(BF16))
        k_tail.append((k[rows] * jnp.exp(g_last - big_g)).astype(BF16))
        decay.append(jnp.exp(g_last))
        vb.append(v[rows].astype(BF16))
    causal = (lax.broadcasted_iota(jnp.int32, (c, c), 0)
              >= lax.broadcasted_iota(jnp.int32, (c, c), 1))
    pairs = [(ci, h) for ci in range(nc) for h in range(n_heads)]
    ks = lambda h: slice(h * dk, (h + 1) * dk)
    vs = lambda h: slice(h * dv, (h + 1) * dv)
    scores = {p: _mm_nt(q_dec[p[0]][:, ks(p[1])], k_dec[p[0]][:, ks(p[1])]) for p in pairs}
    update = {p: _mm_tn(vb[p[0]][:, vs(p[1])], k_tail[p[0]][:, ks(p[1])]) for p in pairs}
    scores = {p: jnp.where(causal, s, 0.0).astype(BF16) for p, s in scores.items()}
    intra = {p: _mm(scores[p], vb[p[0]][:, vs(p[1])]) for p in pairs}
    entering = {}
    for h in range(n_heads):
        state = st_ref[h]
        for ci in range(nc):
            entering[ci, h] = state.astype(BF16)
            state = state * decay[ci][:, ks(h)] + update[ci, h]
        st_ref[h] = state
    inter = {p: _mm_nt(q_state[p[0]][:, ks(p[1])], entering[p]) for p in pairs}
    return jnp.concatenate(
        [jnp.concatenate([post(intra[ci, h] + inter[ci, h], h, ci) for h in range(n_heads)], axis=1)
         for ci in range(nc)], axis=0)


def _hgrn_lower_bound(lbl_ref, layer):
    logits = lbl_ref[...]
    e = jnp.exp(logits - jnp.max(logits, axis=0, keepdims=True))
    p = e / jnp.sum(e, axis=0, keepdims=True)
    return jnp.sum(p[0:layer + 1, :], axis=0, keepdims=True) - p[0:1, :]


def _hgrn_gates(p, lb, d):
    forget = lb + (1.0 - lb) * jax.nn.sigmoid(p[:, d:2 * d])
    return _silu(p[:, :d]), 1.0 - forget, p[:, 2 * d:], jnp.log(forget)


def _gla_gates(p, gk, wgk2_ref, bgk2_ref, d):
    key = d // 2
    z = _mm(gk.astype(BF16), wgk2_ref[...]) + bgk2_ref[...]
    log_sig = jnp.minimum(z, 0.0) - jnp.log(1.0 + jnp.exp(-jnp.abs(z)))
    q = p[:, :key] * ((key // GLA_HEADS) ** -0.5)
    return q, p[:, key:2 * key], p[:, 2 * key:2 * key + d], log_sig / GLA_GATE_NORM


def _recurrent_prompt_body(kind, layer, *refs):
    if kind == "hgrn":
        (x_ref, win_ref, lbl_ref, ng_ref, wout_ref, g_ref, b_ref,
         o_ref, sfin_ref, st_ref) = refs
    else:
        (x_ref, win_ref, wgk_ref, wgk2_ref, bgk2_ref, ng_ref, wout_ref, g_ref, b_ref,
         o_ref, sfin_ref, st_ref) = refs
    t = pl.program_id(1)
    tm, d = x_ref.shape
    n_heads, dv, dk = st_ref.shape

    @pl.when(t == 0)
    def _reset():
        st_ref[...] = jnp.zeros(st_ref.shape, F32)

    x = x_ref[...]
    xb = x.astype(BF16)
    p = _mm(xb, win_ref[...])
    ng = ng_ref[...]
    chunk = min(REC_CHUNK, tm)
    if kind == "hgrn":
        q, k, v, g = _hgrn_gates(p, _hgrn_lower_bound(lbl_ref, layer), d)
        post = lambda o, h, ci: _rms_norm(o, ng)
    else:
        q, k, v, g = _gla_gates(p, _mm(xb, wgk_ref[...]), wgk2_ref, bgk2_ref, d)
        gate = _silu(p[:, 2 * d:3 * d])
        post = lambda o, h, ci: (_rms_norm(o, ng)
                                 * gate[ci * chunk:(ci + 1) * chunk, h * dv:(h + 1) * dv])
    on = _recurrence_tile(q, k, v, g, st_ref, n_heads, dk, dv, chunk, post)
    y = _mm(on.astype(BF16), wout_ref[...])
    o_ref[...] = _layer_norm(ALPHA * x + y, g_ref[...], b_ref[...])

    @pl.when(t == pl.num_programs(1) - 1)
    def _state():
        for h in range(n_heads):
            sfin_ref[0, h] = st_ref[h].T


def _recurrent_prompt(kind, layer, x, n_seq, n_heads, dk, dv, weights, w_out, g, b):
    t, d = x.shape
    seq = t // n_seq
    tm = min(MIX_TILE, seq)
    nt = seq // tm
    row = lambda s, i: (s * nt + i, 0)
    w_specs = [_const_spec(w.shape) for w in weights]
    return pl.pallas_call(
        functools.partial(_recurrent_prompt_body, kind, layer),
        grid=(n_seq, nt),
        in_specs=[pl.BlockSpec((tm, d), row)] + w_specs + [
            _const_spec((d, d)), _const_spec((1, d)), _const_spec((1, d))],
        out_specs=[
            pl.BlockSpec((tm, d), row),
            pl.BlockSpec((1, n_heads, dk, dv), lambda s, i: (s, 0, 0, 0)),
        ],
        out_shape=[
            jax.ShapeDtypeStruct((t, d), F32),
            jax.ShapeDtypeStruct((n_seq, n_heads, dk, dv), F32),
        ],
        scratch_shapes=[pltpu.VMEM((n_heads, dv, dk), F32)],
        compiler_params=_params(2),
        name=kind + "_prompt",
    )(x, *weights, w_out, g, b)


def _recurrent_sample_body(kind, layer, *refs):
    if kind == "hgrn":
        (x_ref, s_ref, win_ref, lbl_ref, ng_ref, wout_ref, g_ref, b_ref,
         o_ref, snew_ref, q_scr, k_scr, v_scr, f_scr, gate_scr, on_scr) = refs
    else:
        (x_ref, s_ref, win_ref, wgk_ref, wgk2_ref, bgk2_ref, ng_ref, wout_ref, g_ref, b_ref,
         o_ref, snew_ref, q_scr, k_scr, v_scr, f_scr, gate_scr, on_scr) = refs
    n = pl.program_id(0)
    d = x_ref.shape[1]
    _, n_heads, dk, dv = s_ref.shape

    @pl.when(n == 0)
    def _project():
        xb = x_ref[...].astype(BF16)
        p = _mm(xb, win_ref[...])
        if kind == "hgrn":
            q, k, v, g = _hgrn_gates(p, _hgrn_lower_bound(lbl_ref, layer), d)
            gate_scr[...] = jnp.zeros(gate_scr.shape, F32)
        else:
            q, k, v, g = _gla_gates(p, _mm(xb, wgk_ref[...]), wgk2_ref, bgk2_ref, d)
            gate_scr[...] = p[:, 2 * d:3 * d]
        q_scr[...] = q
        k_scr[...] = k
        v_scr[...] = v
        f_scr[...] = jnp.exp(g)

    eye = _identity_bf16()
    q = q_scr[pl.ds(n, 1), :]
    k = k_scr[pl.ds(n, 1), :]
    v = v_scr[pl.ds(n, 1), :]
    f = f_scr[pl.ds(n, 1), :]
    f_hi = f.astype(BF16)
    f_r = f - f_hi.astype(F32)
    f_mid = f_r.astype(BF16)
    f_lo = (f_r - f_mid.astype(F32)).astype(BF16)
    f_col = (_lane_broadcast_columns(f_hi, eye) + _lane_broadcast_columns(f_mid, eye)
             + _lane_broadcast_columns(f_lo, eye))
    k_col = _lane_broadcast_columns(k.astype(BF16), eye)
    reps = dv // LANES
    widen = lambda a: a if reps == 1 else jnp.concatenate([a] * reps, axis=1)
    outs = []
    for h in range(n_heads):
        ks = slice(h * dk, (h + 1) * dk)
        vs = slice(h * dv, (h + 1) * dv)
        s_new = widen(f_col[ks]) * s_ref[0, h] + widen(k_col[ks]) * v[:, vs]
        snew_ref[0, h] = s_new
        q_rows = jnp.broadcast_to(q[:, ks], (2 * SUBLANES, dk)).astype(BF16)
        outs.append(_mm(q_rows, s_new.astype(BF16))[0:1, :])
    on_scr[pl.ds(n, 1), :] = jnp.concatenate(outs, axis=1)

    @pl.when(n == pl.num_programs(0) - 1)
    def _finish():
        ng = ng_ref[...]
        o = on_scr[...]
        parts = []
        for h in range(n_heads):
            oh = _rms_norm(o[:, h * dv:(h + 1) * dv], ng)
            if kind == "gla":
                oh = oh * _silu(gate_scr[:, h * dv:(h + 1) * dv])
            parts.append(oh)
        y = _mm(jnp.concatenate(parts, axis=1).astype(BF16), wout_ref[...])
        o_ref[...] = _layer_norm(ALPHA * x_ref[...] + y, g_ref[...], b_ref[...])


def _recurrent_sample(kind, layer, x, state, weights, w_out, g, b):
    n, d = x.shape
    _, n_heads, dk, dv = state.shape
    w_specs = [_const_spec(w.shape) for w in weights]
    st_spec = pl.BlockSpec((1, n_heads, dk, dv), lambda s: (s, 0, 0, 0))
    return pl.pallas_call(
        functools.partial(_recurrent_sample_body, kind, layer),
        grid=(n,),
        in_specs=[_const_spec((n, d)), st_spec] + w_specs + [
            _const_spec((d, d)), _const_spec((1, d)), _const_spec((1, d))],
        out_specs=[_const_spec((n, d)), st_spec],
        out_shape=[jax.ShapeDtypeStruct((n, d), F32), jax.ShapeDtypeStruct(state.shape, F32)],
        scratch_shapes=[
            pltpu.VMEM((n, n_heads * dk), F32),
            pltpu.VMEM((n, n_heads * dk), F32),
            pltpu.VMEM((n, n_heads * dv), F32),
            pltpu.VMEM((n, n_heads * dk), F32),
            pltpu.VMEM((n, d), F32),
            pltpu.VMEM((n, n_heads * dv), F32),
        ],
        compiler_params=_params(1),
        name=kind + "_sample",
    )(x, state, *weights, w_out, g, b)


def kernel(x_prompt, x_sample, state_conv, cache_k, cache_v, page_table, state_hgrn, state_gla,
           ffn_w_up, ffn_w_down, ln_g, ln_b,
           conv_w_in, conv_w, conv_w_out,
           da_w_in, da_lambda_q1, da_lambda_k1, da_lambda_q2, da_lambda_k2, da_subln_g, da_w_out,
           hg_w_in, hg_lb_logits, hg_norm_g, hg_w_out,
           gla_w_in, gla_w_gk2, gla_b_gk2, gla_norm_g, gla_w_out):
    n_seq, seq, d = x_prompt.shape
    n_smp = x_sample.shape[0]
    xp = x_prompt.reshape(n_seq * seq, d)
    xs = x_sample.reshape(n_smp, d)
    bf = lambda w: w.astype(BF16)
    row = lambda v: v.reshape(1, -1)
    w_up, w_down = bf(ffn_w_up), bf(ffn_w_down)

    def ffn_pair(i, half, xp, xs):
        g, b = row(ln_g[i, half * 2]), row(ln_b[i, half * 2])
        return _ffn(xp, xs, w_up, w_down, i, half, g, b)

    xp, xs = ffn_pair(0, 0, xp, xs)
    g, b = row(ln_g[0, 1]), row(ln_b[0, 1])
    cw_in, cw_out = bf(conv_w_in), bf(conv_w_out)
    xp, conv_p = _conv_prompt(xp, n_seq, cw_in, conv_w, cw_out, g, b)
    xs, u_s = _conv_sample(xs, state_conv[:, 0], state_conv[:, 1], cw_in, conv_w, cw_out, g, b)
    conv_s = jnp.stack([state_conv[:, 1], u_s], axis=1)
    xp, xs = ffn_pair(0, 1, xp, xs)

    xp, xs = ffn_pair(1, 0, xp, xs)
    g, b = row(ln_g[1, 1]), row(ln_b[1, 1])
    lams = [row(v) for v in (da_lambda_q1, da_lambda_k1, da_lambda_q2, da_lambda_k2)]
    subln = row(da_subln_g)
    dw_in, dw_out = bf(da_w_in), bf(da_w_out)
    past = page_table.shape[1] * PAGE_SIZE
    qt, kt, kb, v, vt = _da_qkv_prompt(
        xp, n_seq, dw_in[:, :d].T, dw_in[:, d:2 * d].T, dw_in[:, 2 * d:],
        _rope_tables(jnp.arange(seq, dtype=jnp.int32)))
    on = _flash_prompt(qt, kb, vt, lams, da_subln_g.reshape(-1, 1))
    k_p = kt.reshape(n_seq, 2 * DA_HEADS, DA_HD, seq).transpose(0, 3, 1, 2)
    v_p = v.reshape(n_seq, seq, DA_HEADS, 2 * DA_HD)
    xp = _proj_norm(on, xp, dw_out, g, b)
    qs, ks, vs = _da_qkv_sample(xs, dw_in, _rope_tables(jnp.full((1,), past, jnp.int32)))
    ons = _decode_attn(qs, ks, vs, cache_k.transpose(0, 2, 3, 1), cache_v, page_table, lams, subln)
    k_s = ks.reshape(n_smp, 1, 2 * DA_HEADS, DA_HD)
    v_s = vs.reshape(n_smp, 1, DA_HEADS, 2 * DA_HD)
    xs = _proj_norm(ons.reshape(n_smp, d).astype(BF16), xs, dw_out, g, b)
    xp, xs = ffn_pair(1, 1, xp, xs)

    xp, xs = ffn_pair(2, 0, xp, xs)
    g, b = row(ln_g[2, 1]), row(ln_b[2, 1])
    hg_weights = (bf(hg_w_in), hg_lb_logits, row(hg_norm_g))
    dk = d // HG_HEADS
    xp, hg_p = _recurrent_prompt("hgrn", 2, xp, n_seq, HG_HEADS, dk, dk, hg_weights,
                                 bf(hg_w_out), g, b)
    xs, hg_s = _recurrent_sample("hgrn", 2, xs, state_hgrn, hg_weights, bf(hg_w_out), g, b)
    xp, xs = ffn_pair(2, 1, xp, xs)

    xp, xs = ffn_pair(3, 0, xp, xs)
    g, b = row(ln_g[3, 1]), row(ln_b[3, 1])
    main = 3 * d
    w_gk = jnp.pad(bf(gla_w_in[:, main:]), ((0, 0), (0, LANES - GLA_LOWRANK)))
    w_gk2 = jnp.pad(bf(gla_w_gk2), ((0, LANES - GLA_LOWRANK), (0, 0)))
    gla_weights = (bf(gla_w_in[:, :main]), w_gk, w_gk2, row(gla_b_gk2), row(gla_norm_g))
    gdk = d // 2 // GLA_HEADS
    gdv = d // GLA_HEADS
    xp, gla_p = _recurrent_prompt("gla", 3, xp, n_seq, GLA_HEADS, gdk, gdv, gla_weights,
                                  bf(gla_w_out), g, b)
    xs, gla_s = _recurrent_sample("gla", 3, xs, state_gla, gla_weights, bf(gla_w_out), g, b)
    xp, xs = ffn_pair(3, 1, xp, xs)

    return (xp.reshape(n_seq, seq, d), xs.reshape(n_smp, 1, d), conv_p, conv_s,
            k_p, v_p, k_s, v_s, hg_p, hg_s, gla_p, gla_s)
```

```python
import functools
import math

import jax
import jax.numpy as jnp
from jax import lax
from jax.experimental import pallas as pl
from jax.experimental.pallas import tpu as pltpu

F32 = jnp.float32
BF16 = jnp.bfloat16

DEPTH = 4
LN_EPS = 1e-5
RMS_EPS = 1e-6
ALPHA = (2.0 * DEPTH) ** 0.25
CONV_W = 3
DA_HEADS = 8
DA_HD = 64
ROT_DIM = DA_HD // 4
ROPE_THETA = 500000.0
DA_LAMBDA_INIT = 0.8 - 0.6 * math.exp(-0.3 * 1)
NEG_INF = -1e30
PAGE_SIZE = 128
HG_HEADS = 8
GLA_HEADS = 4
GLA_LOWRANK = 16
GLA_GATE_NORM = 16.0

LANES = 128
SUBLANES = 8
MXU_DIM = 256
VMEM_LIMIT = 56 * 1024 * 1024

TOKEN_TILE = 512
FFN_TILE = 512
FFN_CHUNK = 1408
MIX_TILE = 256
REC_CHUNK = 64
ATT_TILE = 512
DECODE_PAGES_PER_STEP = 8


def _mm(a, b):
    return jnp.dot(a, b, preferred_element_type=F32)


def _mm_nt(a, b):
    return lax.dot_general(a, b, (((1,), (1,)), ((), ())), preferred_element_type=F32)


def _mm_tn(a, b):
    return lax.dot_general(a, b, (((0,), (0,)), ((), ())), preferred_element_type=F32)


def _layer_norm(z, g, b):
    mu = jnp.mean(z, axis=-1, keepdims=True)
    zc = z - mu
    var = jnp.mean(zc * zc, axis=-1, keepdims=True)
    return zc * lax.rsqrt(var + LN_EPS) * g + b


def _rms_norm(o, g):
    return o * lax.rsqrt(jnp.mean(o * o, axis=-1, keepdims=True) + RMS_EPS) * g


def _silu(a):
    return a * jax.nn.sigmoid(a)


def _params(n_axes):
    return pltpu.CompilerParams(dimension_semantics=("arbitrary",) * n_axes,
                                vmem_limit_bytes=VMEM_LIMIT)


def _const_spec(shape):
    nd = len(shape)
    return pl.BlockSpec(shape, lambda *_: (0,) * nd)


def _ffn_chunks(f):
    n = max(1, f // FFN_CHUNK)
    edges = [MXU_DIM * round(i * f / n / MXU_DIM) for i in range(n)] + [f]
    return list(zip(edges[:-1], edges[1:]))


def _ffn_body(nt, x_ref, xs_ref, wu_ref, wd_ref, g_ref, b_ref, o_ref, os_ref, z_ref):
    i = pl.program_id(0)
    f = wd_ref.shape[0]
    g = g_ref[...]
    b = b_ref[...]

    def residual_plus_ffn(x, norm_of=None):
        xb = x.astype(BF16)
        acc = None
        normed = None
        for c0, c1 in _ffn_chunks(f):
            a = _mm(xb, wu_ref[:, c0:c1])
            u = _mm(xb, wu_ref[:, f + c0:f + c1])
            if norm_of is not None and normed is None:
                normed = _layer_norm(norm_of, g, b)
            h = (_silu(a) * u).astype(BF16)
            part = _mm(h, wd_ref[c0:c1, :])
            acc = part if acc is None else acc + part
        return ALPHA * x + 0.5 * acc, normed

    @pl.when(i == 0)
    def _init():
        z_ref[...] = jnp.zeros(z_ref.shape, F32)

    @pl.when(i < nt)
    def _tile():
        z_new, normed = residual_plus_ffn(x_ref[...], z_ref[...])
        o_ref[...] = normed
        z_ref[...] = z_new

    @pl.when(i == nt)
    def _tail():
        o_ref[...] = _layer_norm(z_ref[...], g, b)
        zs, _ = residual_plus_ffn(xs_ref[...])
        os_ref[...] = _layer_norm(zs, g, b)


def _ffn(x, xs, w_up, w_down, layer, half, g, b):
    t, d = x.shape
    ns = xs.shape[0]
    f = w_down.shape[2]
    tm = min(FFN_TILE, t)
    nt = t // tm
    once = pl.Buffered(1)
    return pl.pallas_call(
        functools.partial(_ffn_body, nt),
        grid=(nt + 1,),
        in_specs=[
            pl.BlockSpec((tm, d), lambda i: (jnp.minimum(i, nt - 1), 0)),
            _const_spec((ns, d)),
            pl.BlockSpec((None, None, d, 2 * f), lambda i: (layer, half, 0, 0), pipeline_mode=once),
            pl.BlockSpec((None, None, f, d), lambda i: (layer, half, 0, 0), pipeline_mode=once),
            _const_spec((1, d)),
            _const_spec((1, d)),
        ],
        out_specs=[
            pl.BlockSpec((tm, d), lambda i: (jnp.maximum(i - 1, 0), 0)),
            _const_spec((ns, d)),
        ],
        out_shape=[jax.ShapeDtypeStruct((t, d), F32), jax.ShapeDtypeStruct((ns, d), F32)],
        scratch_shapes=[pltpu.VMEM((tm, d), F32)],
        compiler_params=_params(1),
        name="ffn",
    )(x, xs, w_up, w_down, g, b)


def _proj_norm_body(y_ref, x_ref, w_ref, g_ref, b_ref, o_ref):
    y = _mm(y_ref[...], w_ref[...])
    o_ref[...] = _layer_norm(ALPHA * x_ref[...] + y, g_ref[...], b_ref[...])


def _proj_norm(y, x, w, g, b):
    t, d = x.shape
    tm = min(TOKEN_TILE, t)
    return pl.pallas_call(
        _proj_norm_body,
        grid=(t // tm,),
        in_specs=[
            pl.BlockSpec((tm, d), lambda i: (i, 0)),
            pl.BlockSpec((tm, d), lambda i: (i, 0)),
            _const_spec((d, d)),
            _const_spec((1, d)),
            _const_spec((1, d)),
        ],
        out_specs=pl.BlockSpec((tm, d), lambda i: (i, 0)),
        out_shape=jax.ShapeDtypeStruct((t, d), F32),
        compiler_params=_params(1),
        name="proj_norm",
    )(y, x, w, g, b)


def _conv_prompt_body(x_ref, win_ref, cw_ref, wout_ref, g_ref, b_ref, o_ref, st_ref, ubuf_ref):
    t = pl.program_id(1)
    tm, d = x_ref.shape

    @pl.when(t == 0)
    def _reset():
        ubuf_ref[0:SUBLANES, :] = jnp.zeros((SUBLANES, d), F32)

    x = x_ref[...]
    p = _mm(x.astype(BF16), win_ref[...])
    u = p[:, d:2 * d] * p[:, 2 * d:]
    ubuf_ref[SUBLANES:, :] = u
    ue = ubuf_ref[...]
    um1 = pltpu.roll(ue, 1, 0)[SUBLANES:, :]
    um2 = pltpu.roll(ue, 2, 0)[SUBLANES:, :]
    cw = cw_ref[...]
    conv = cw[0:1, :] * um2 + cw[1:2, :] * um1 + cw[2:3, :] * u
    y = _mm((p[:, :d] * conv).astype(BF16), wout_ref[...])
    o_ref[...] = _layer_norm(ALPHA * x + y, g_ref[...], b_ref[...])
    ubuf_ref[0:SUBLANES, :] = ubuf_ref[tm:tm + SUBLANES, :]

    @pl.when(t == pl.num_programs(1) - 1)
    def _state():
        st_ref[0] = ubuf_ref[tm + SUBLANES - (CONV_W - 1):tm + SUBLANES, :]


def _conv_prompt(x, n_seq, w_in, conv_w, w_out, g, b):
    t, d = x.shape
    seq = t // n_seq
    tm = min(TOKEN_TILE, seq)
    nt = seq // tm
    return pl.pallas_call(
        _conv_prompt_body,
        grid=(n_seq, nt),
        in_specs=[
            pl.BlockSpec((tm, d), lambda s, i: (s * nt + i, 0)),
            _const_spec((d, 3 * d)),
            _const_spec((CONV_W, d)),
            _const_spec((d, d)),
            _const_spec((1, d)),
            _const_spec((1, d)),
        ],
        out_specs=[
            pl.BlockSpec((tm, d), lambda s, i: (s * nt + i, 0)),
            pl.BlockSpec((1, CONV_W - 1, d), lambda s, i: (s, 0, 0)),
        ],
        out_shape=[
            jax.ShapeDtypeStruct((t, d), F32),
            jax.ShapeDtypeStruct((n_seq, CONV_W - 1, d), F32),
        ],
        scratch_shapes=[pltpu.VMEM((tm + SUBLANES, d), F32)],
        compiler_params=_params(2),
        name="conv_prompt",
    )(x, w_in, conv_w, w_out, g, b)


def _conv_sample_body(x_ref, s0_ref, s1_ref, win_ref, cw_ref, wout_ref, g_ref, b_ref,
                      o_ref, u_ref):
    d = x_ref.shape[1]
    x = x_ref[...]
    p = _mm(x.astype(BF16), win_ref[...])
    u = p[:, d:2 * d] * p[:, 2 * d:]
    cw = cw_ref[...]
    conv = cw[0:1, :] * s0_ref[...] + cw[1:2, :] * s1_ref[...] + cw[2:3, :] * u
    y = _mm((p[:, :d] * conv).astype(BF16), wout_ref[...])
    o_ref[...] = _layer_norm(ALPHA * x + y, g_ref[...], b_ref[...])
    u_ref[...] = u


def _conv_sample(x, s0, s1, w_in, conv_w, w_out, g, b):
    n, d = x.shape
    return pl.pallas_call(
        _conv_sample_body,
        out_shape=[jax.ShapeDtypeStruct((n, d), F32), jax.ShapeDtypeStruct((n, d), F32)],
        compiler_params=pltpu.CompilerParams(vmem_limit_bytes=VMEM_LIMIT),
        name="conv_sample",
    )(x, s0, s1, w_in, conv_w, w_out, g, b)


def _rope_tables(pos):
    inv_freq = 1.0 / (ROPE_THETA ** (jnp.arange(0, ROT_DIM, 2, dtype=F32) / ROT_DIM))
    ang = pos.astype(F32)[:, None] * inv_freq[None, :]
    cos, sin = jnp.cos(ang), jnp.sin(ang)
    half = ROT_DIM // 2
    n = pos.shape[0]
    rest = DA_HD - ROT_DIM
    c64 = jnp.concatenate([cos, cos, jnp.ones((n, rest), F32)], axis=1)
    up64 = jnp.concatenate([-sin, jnp.zeros((n, DA_HD - half), F32)], axis=1)
    dn64 = jnp.concatenate([jnp.zeros((n, half), F32), sin, jnp.zeros((n, rest), F32)], axis=1)
    rep = LANES // DA_HD
    return (jnp.tile(c64, (1, rep)), jnp.tile(up64, (1, rep)), jnp.tile(dn64, (1, rep)),
            cos.T, sin.T)


def _rope_rows(x, c, s_up, s_dn):
    half = ROT_DIM // 2
    blocks = []
    for i in range(x.shape[1] // LANES):
        xb = x[:, i * LANES:(i + 1) * LANES]
        blocks.append(xb * c + pltpu.roll(xb, LANES - half, 1) * s_up + pltpu.roll(xb, half, 1) * s_dn)
    return jnp.concatenate(blocks, axis=1)


def _rope_cols(xt, ct, st):
    d, cols = xt.shape
    half = ROT_DIM // 2
    x3 = xt.reshape(d // DA_HD, DA_HD, cols)
    x1 = x3[:, 0:half, :]
    x2 = x3[:, half:ROT_DIM, :]
    x3 = jnp.concatenate([x1 * ct - x2 * st, x2 * ct + x1 * st, x3[:, ROT_DIM:, :]], axis=1)
    return x3.reshape(d, cols)


def _da_qkv_prompt_body(x_ref, wqt_ref, wkt_ref, wv_ref, ct_ref, st_ref,
                        qt_ref, kt_ref, kb_ref, v_ref, vt_ref):
    xb = x_ref[...].astype(BF16)
    ct = ct_ref[...]
    st = st_ref[...]
    qt = _rope_cols(_mm_nt(wqt_ref[...], xb), ct, st)
    qt_ref[0] = (qt * (DA_HD ** -0.5)).astype(BF16)
    kt = _rope_cols(_mm_nt(wkt_ref[...], xb), ct, st)
    kt_ref[0] = kt
    kb_ref[...] = kt.T.astype(BF16)
    v = _mm(xb, wv_ref[...])
    v_ref[...] = v
    vt_ref[0, 0] = v.T.astype(BF16)


def _da_qkv_prompt(x, n_seq, wqt, wkt, wv, tables):
    t, d = x.shape
    seq = t // n_seq
    tm = min(ATT_TILE, seq)
    nt = seq // tm
    ct, st = tables[3:]
    half = ROT_DIM // 2
    row = lambda s, i: (s * nt + i, 0)
    return pl.pallas_call(
        _da_qkv_prompt_body,
        grid=(n_seq, nt),
        in_specs=[
            pl.BlockSpec((tm, d), row),
            _const_spec((d, d)),
            _const_spec((d, d)),
            _const_spec((d, d)),
            pl.BlockSpec((half, tm), lambda s, i: (0, i)),
            pl.BlockSpec((half, tm), lambda s, i: (0, i)),
        ],
        out_specs=[
            pl.BlockSpec((1, d, tm), lambda s, i: (s, 0, i)),
            pl.BlockSpec((1, d, tm), lambda s, i: (s, 0, i)),
            pl.BlockSpec((tm, d), row),
            pl.BlockSpec((tm, d), row),
            pl.BlockSpec((1, 1, d, tm), lambda s, i: (s, i, 0, 0)),
        ],
        out_shape=[
            jax.ShapeDtypeStruct((n_seq, d, seq), BF16),
            jax.ShapeDtypeStruct((n_seq, d, seq), F32),
            jax.ShapeDtypeStruct((t, d), BF16),
            jax.ShapeDtypeStruct((t, d), F32),
            jax.ShapeDtypeStruct((n_seq, nt, d, tm), BF16),
        ],
        compiler_params=_params(2),
        name="da_qkv_prompt",
    )(x, wqt, wkt, wv, ct, st)


def _diff_lambda(lq1_ref, lk1_ref, lq2_ref, lk2_ref):
    s1 = jnp.sum(lq1_ref[...] * lk1_ref[...], axis=-1, keepdims=True)
    s2 = jnp.sum(lq2_ref[...] * lk2_ref[...], axis=-1, keepdims=True)
    return jnp.exp(s1) - jnp.exp(s2) + DA_LAMBDA_INIT


def _flash_body(lq1_ref, lk1_ref, lq2_ref, lk2_ref, *refs):
    _flash_step(pl.program_id(2), (lq1_ref, lk1_ref, lq2_ref, lk2_ref), *refs)


def _flash_step(qi, lam_refs, qt_ref, k_ref, vt_ref, sg_ref, o_ref, q2_ref, m_ref, l_ref, acc_ref):
    tq = qt_ref.shape[2]
    tk = vt_ref.shape[3]
    n_col = 2 * tq
    cb = min(2 * LANES, tq)

    qt = qt_ref[0]
    sub = lax.broadcasted_iota(jnp.int32, qt.shape, 0)
    zero = jnp.zeros_like(qt)
    q2_ref[:, 0:tq] = jnp.where(sub < DA_HD, qt, zero)
    q2_ref[:, tq:] = jnp.where(sub >= DA_HD, qt, zero)
    m_ref[...] = jnp.full(m_ref.shape, NEG_INF, F32)
    l_ref[...] = jnp.zeros(l_ref.shape, F32)
    acc_ref[...] = jnp.zeros(acc_ref.shape, F32)

    starts = list(range(0, n_col, cb))

    def kv_tiles(kjs, diagonal):
        k_tiles = [k_ref[pl.ds(pl.multiple_of(kj * tk, tk), tk), :] for kj in kjs]
        vt_tiles = [vt_ref[0, kj] for kj in kjs]
        items = [(t, c0) for t in range(len(kjs)) for c0 in starts]
        n_items = len(items)

        def score(n):
            t, c0 = items[n]
            return _mm(k_tiles[t], q2_ref[:, c0:c0 + cb])

        def max_pass(n, st):
            c0 = items[n][1]
            if diagonal:
                key = lax.broadcasted_iota(jnp.int32, st.shape, 0)
                qry = lax.broadcasted_iota(jnp.int32, st.shape, 1) + (c0 % tq)
                st = jnp.where(key <= qry, st, NEG_INF)
            m_prev = m_ref[:, c0:c0 + cb]
            return st, m_prev, jnp.maximum(m_prev, jnp.max(st, axis=0, keepdims=True))

        def exp_pass(n, st, m_prev, m_new):
            t, c0 = items[n]
            cs = slice(c0, c0 + cb)
            alpha = jnp.exp(m_prev - m_new)
            pt = jnp.exp(st - m_new)
            l_ref[:, cs] = alpha * l_ref[:, cs] + jnp.sum(pt, axis=0, keepdims=True)
            acc_ref[:, cs] = alpha * acc_ref[:, cs] + _mm(vt_tiles[t], pt.astype(BF16))
            m_ref[:, cs] = m_new

        raw = {n: score(n) for n in range(min(2, n_items))}
        ready = {0: max_pass(0, raw.pop(0))}
        for n in range(n_items):
            if n + 2 < n_items:
                raw[n + 2] = score(n + 2)
            if n + 1 < n_items:
                ready[n + 1] = max_pass(n + 1, raw.pop(n + 1))
            exp_pass(n, *ready.pop(n))

    def tile_pair(jj, carry):
        kv_tiles([2 * jj, 2 * jj + 1], False)
        return carry

    lax.fori_loop(0, lax.shift_right_logical(qi, 1), tile_pair, 0)

    @pl.when(jnp.bitwise_and(qi, 1) == 1)
    def _odd_tile():
        kv_tiles([qi - 1], False)

    kv_tiles([qi], True)

    lam = _diff_lambda(*lam_refs)
    o = acc_ref[...] / l_ref[...]
    o = o[:, 0:tq] - lam * o[:, tq:]
    r = lax.rsqrt(jnp.mean(o * o, axis=0, keepdims=True) + RMS_EPS)
    o = o * r * sg_ref[...] * (1.0 - DA_LAMBDA_INIT)
    o_ref[...] = o.T.astype(BF16)


def _flash_prompt(qt, kb, vt, lams, subln_col):
    n_seq, d, seq = qt.shape
    nq, ta = vt.shape[1], vt.shape[3]
    hw = 2 * DA_HD
    return pl.pallas_call(
        _flash_body,
        grid=(n_seq, DA_HEADS, nq),
        in_specs=[_const_spec((1, DA_HD))] * 4 + [
            pl.BlockSpec((1, hw, ta), lambda s, h, i: (s, h, i)),
            pl.BlockSpec((seq, hw), lambda s, h, i: (s, h)),
            pl.BlockSpec((1, nq, hw, ta), lambda s, h, i: (s, 0, h, 0)),
            _const_spec((hw, 1)),
        ],
        out_specs=pl.BlockSpec((ta, hw), lambda s, h, i: (s * nq + i, h)),
        out_shape=jax.ShapeDtypeStruct((n_seq * seq, d), BF16),
        scratch_shapes=[
            pltpu.VMEM((hw, 2 * ta), BF16),
            pltpu.VMEM((1, 2 * ta), F32),
            pltpu.VMEM((1, 2 * ta), F32),
            pltpu.VMEM((hw, 2 * ta), F32),
        ],
        compiler_params=_params(3),
        name="flash_prompt",
    )(*lams, qt, kb, vt, subln_col)


def _da_qkv_sample_body(x_ref, w_ref, c_ref, up_ref, dn_ref, q_ref, k_ref, v_ref):
    d = x_ref.shape[1]
    p = _mm(x_ref[...].astype(BF16), w_ref[...])
    c, up, dn = c_ref[...], up_ref[...], dn_ref[...]
    q_ref[...] = _rope_rows(p[:, :d], c, up, dn) * (DA_HD ** -0.5)
    k_ref[...] = _rope_rows(p[:, d:2 * d], c, up, dn)
    v_ref[...] = p[:, 2 * d:]


def _da_qkv_sample(x, w_in, tables):
    n, d = x.shape
    c, up, dn = tables[:3]
    return pl.pallas_call(
        _da_qkv_sample_body,
        out_shape=[jax.ShapeDtypeStruct((n, d), F32)] * 3,
        compiler_params=pltpu.CompilerParams(vmem_limit_bytes=VMEM_LIMIT),
        name="da_qkv_sample",
    )(x, w_in, c, up, dn)


def _lane_broadcast_columns(vec_bf, eye):
    w = vec_bf.shape[1]
    rows = jnp.broadcast_to(vec_bf, (LANES, w))
    return jnp.concatenate(
        [_mm_nt(eye, rows[:, i * LANES:(i + 1) * LANES]) for i in range(w // LANES)], axis=0)


def _identity_bf16():
    r = lax.broadcasted_iota(jnp.int32, (LANES, LANES), 0)
    c = lax.broadcasted_iota(jnp.int32, (LANES, LANES), 1)
    return jnp.where(r == c, 1.0, 0.0).astype(BF16)


def _decode_attn_body(npg, pt_ref, lq1_ref, lk1_ref, lq2_ref, lk2_ref, q_ref, kn_ref, vn_ref, *rest):
    del pt_ref
    _decode_step(pl.program_id(1), pl.num_programs(1) - 1, (lq1_ref, lk1_ref, lq2_ref, lk2_ref),
                 q_ref, kn_ref, vn_ref, rest[:npg], rest[npg:2 * npg], *rest[2 * npg:])


def _decode_step(j, j_last, lam_refs, q_ref, kn_ref, vn_ref, kt_refs, v_refs,
                 sg_ref, o_ref, qb_ref, ex_ref, m_ref, l_ref, acc_ref):
    npg = len(kt_refs)
    n_comp = 2 * DA_HEADS
    flat = PAGE_SIZE * DA_HEADS

    @pl.when(j == 0)
    def _init():
        qb = _lane_broadcast_columns(q_ref[0].astype(BF16), _identity_bf16())
        qb_ref[...] = qb.reshape(qb_ref.shape)
        key = lax.broadcasted_iota(jnp.int32, (PAGE_SIZE, flat), 0)
        col = lax.broadcasted_iota(jnp.int32, (PAGE_SIZE, flat), 1)
        ex_ref[...] = jnp.where(col // DA_HEADS == key, 1.0, 0.0).astype(BF16)
        m_ref[...] = jnp.full(m_ref.shape, NEG_INF, F32)
        l_ref[...] = jnp.zeros(l_ref.shape, F32)
        acc_ref[...] = jnp.zeros(acc_ref.shape, F32)

    def scores(kt_of_comp):
        parts = []
        for comp in range(2):
            qb = qb_ref[pl.ds(comp, DA_HEADS, stride=2)]
            parts.append(jnp.sum(qb * kt_of_comp(comp), axis=1))
        return jnp.concatenate(parts, axis=0)

    def online_update(s, pv_of_p):
        m_prev = m_ref[...]
        m_new = jnp.maximum(m_prev, jnp.max(s, axis=-1, keepdims=True))
        alpha = jnp.exp(m_prev - m_new)
        p = jnp.exp(s - m_new)
        l_ref[...] = alpha * l_ref[...] + jnp.sum(p, axis=-1, keepdims=True)
        acc_ref[...] = alpha * acc_ref[...] + pv_of_p(p)
        m_ref[...] = m_new

    def pages_pv(p):
        row = lax.broadcasted_iota(jnp.int32, (n_comp, flat), 0)
        col = lax.broadcasted_iota(jnp.int32, (n_comp, flat), 1)
        own_head = col % DA_HEADS == row % DA_HEADS
        pb = p.astype(BF16)
        ex = ex_ref[...]
        spread = [_mm(pb[:, g * PAGE_SIZE:(g + 1) * PAGE_SIZE], ex) for g in range(npg)]
        spread = [jnp.where(own_head, s, 0.0).astype(BF16) for s in spread]
        parts = [_mm(spread[g], v_refs[g][0].reshape(flat, LANES).astype(BF16))
                 for g in range(npg)]
        return functools.reduce(lambda a, b: a + b, parts)

    s_pages = jnp.concatenate(
        [scores(lambda comp, r=r: r[0, pl.ds(comp, DA_HEADS, stride=2)]) for r in kt_refs], axis=1)
    online_update(s_pages, pages_pv)

    @pl.when(j == j_last)
    def _finish():
        kb = _lane_broadcast_columns(kn_ref[0].astype(BF16), _identity_bf16())
        kb = kb.reshape(DA_HEADS, 2, DA_HD, LANES)
        s_self = scores(lambda comp: kb[:, comp])
        lane = lax.broadcasted_iota(jnp.int32, (n_comp, LANES), 1)
        s_self = jnp.where(lane == 0, s_self, NEG_INF)
        vn = vn_ref[0]
        v_rows = jnp.concatenate(
            [vn[:, h * LANES:(h + 1) * LANES] for h in range(DA_HEADS)] * 2, axis=0)
        online_update(s_self, lambda p: jnp.sum(p, axis=-1, keepdims=True) * v_rows)
        lam = _diff_lambda(*lam_refs)
        o = acc_ref[...] / l_ref[...]
        o = o[0:DA_HEADS, :] - lam * o[DA_HEADS:, :]
        o_ref[0] = _rms_norm(o, sg_ref[...]) * (1.0 - DA_LAMBDA_INIT)


def _decode_attn(q, k_new, v_new, cache_kt, cache_v, page_table, lams, subln_g):
    n, d = q.shape
    n_pages = page_table.shape[1]
    npg = math.gcd(DECODE_PAGES_PER_STEP, n_pages)
    n_comp = 2 * DA_HEADS
    hw = 2 * DA_HD
    row3 = lambda a: a.reshape(n, 1, d)
    vec_spec = pl.BlockSpec((1, 1, d), lambda s, j, pt: (s, 0, 0))
    small = lambda shape: pl.BlockSpec(shape, lambda s, j, pt: (0,) * len(shape))
    page = lambda g: (lambda s, j, pt: (pt[s * n_pages + j * npg + g], 0, 0, 0))
    grid_spec = pltpu.PrefetchScalarGridSpec(
        num_scalar_prefetch=1,
        grid=(n, n_pages // npg),
        in_specs=[small((1, DA_HD))] * 4 + [vec_spec, vec_spec, vec_spec]
        + [pl.BlockSpec((1, n_comp, DA_HD, PAGE_SIZE), page(g)) for g in range(npg)]
        + [pl.BlockSpec((1, PAGE_SIZE, DA_HEADS, hw), page(g)) for g in range(npg)]
        + [small((1, hw))],
        out_specs=pl.BlockSpec((1, DA_HEADS, hw), lambda s, j, pt: (s, 0, 0)),
        scratch_shapes=[
            pltpu.VMEM((n_comp, DA_HD, LANES), F32),
            pltpu.VMEM((PAGE_SIZE, PAGE_SIZE * DA_HEADS), BF16),
            pltpu.VMEM((n_comp, 1), F32),
            pltpu.VMEM((n_comp, 1), F32),
            pltpu.VMEM((n_comp, hw), F32),
        ],
    )
    return pl.pallas_call(
        functools.partial(_decode_attn_body, npg),
        grid_spec=grid_spec,
        out_shape=jax.ShapeDtypeStruct((n, DA_HEADS, hw), F32),
        compiler_params=_params(2),
        name="decode_attn",
    )(page_table.reshape(-1), *lams, row3(q), row3(k_new), row3(v_new),
      *([cache_kt] * npg), *([cache_v] * npg), subln_g)


def _attention_fused_body(nq, n_groups, npg, pt_ref, lq1_ref, lk1_ref, lq2_ref, lk2_ref,
                          qt_ref, k_ref, vt_ref, sgc_ref, q_ref, kn_ref, vn_ref, *rest):
    del pt_ref
    kt_refs, v_refs = rest[:npg], rest[npg:2 * npg]
    (sg_ref, of_ref, od_ref, q2_ref, fm_ref, fl_ref, facc_ref,
     qb_ref, ex_ref, dm_ref, dl_ref, dacc_ref) = rest[2 * npg:]
    step = pl.program_id(0)
    lam_refs = (lq1_ref, lk1_ref, lq2_ref, lk2_ref)
    _flash_step(lax.rem(step, nq), lam_refs, qt_ref, k_ref, vt_ref, sgc_ref, of_ref,
                q2_ref, fm_ref, fl_ref, facc_ref)
    _decode_step(lax.rem(step, n_groups), n_groups - 1, lam_refs, q_ref, kn_ref, vn_ref,
                 kt_refs, v_refs, sg_ref, od_ref, qb_ref, ex_ref, dm_ref, dl_ref, dacc_ref)


def _attention_fused(qt, kb, vt, q, k_new, v_new, cache_kt, cache_v, page_table, lams,
                     subln_col, subln_row):
    n_seq, d, seq = qt.shape
    nq, ta = vt.shape[1], vt.shape[3]
    n = q.shape[0]
    n_pages = page_table.shape[1]
    npg = math.gcd(DECODE_PAGES_PER_STEP, n_pages)
    n_groups = n_pages // npg
    steps = n_seq * DA_HEADS * nq
    assert steps == n * n_groups
    n_comp = 2 * DA_HEADS
    hw = 2 * DA_HD
    f_seq = lambda t: lax.div(t, DA_HEADS * nq)
    f_head = lambda t: lax.rem(lax.div(t, nq), DA_HEADS)
    f_tile = lambda t: lax.rem(t, nq)
    d_seq = lambda t: lax.div(t, n_groups)
    d_grp = lambda t: lax.rem(t, n_groups)
    row3 = lambda a: a.reshape(n, 1, d)
    small = lambda shape: pl.BlockSpec(shape, lambda t, pt: (0,) * len(shape))
    vec_spec = pl.BlockSpec((1, 1, d), lambda t, pt: (d_seq(t), 0, 0))
    page = lambda g: (lambda t, pt: (pt[d_seq(t) * n_pages + d_grp(t) * npg + g], 0, 0, 0))
    grid_spec = pltpu.PrefetchScalarGridSpec(
        num_scalar_prefetch=1,
        grid=(steps,),
        in_specs=[small((1, DA_HD))] * 4 + [
            pl.BlockSpec((1, hw, ta), lambda t, pt: (f_seq(t), f_head(t), f_tile(t))),
            pl.BlockSpec((seq, hw), lambda t, pt: (f_seq(t), f_head(t))),
            pl.BlockSpec((1, nq, hw, ta), lambda t, pt: (f_seq(t), 0, f_head(t), 0)),
            small((hw, 1)),
            vec_spec, vec_spec, vec_spec]
        + [pl.BlockSpec((1, n_comp, DA_HD, PAGE_SIZE), page(g)) for g in range(npg)]
        + [pl.BlockSpec((1, PAGE_SIZE, DA_HEADS, hw), page(g)) for g in range(npg)]
        + [small((1, hw))],
        out_specs=[
            pl.BlockSpec((ta, hw), lambda t, pt: (f_seq(t) * nq + f_tile(t), f_head(t))),
            pl.BlockSpec((1, DA_HEADS, hw), lambda t, pt: (d_seq(t), 0, 0)),
        ],
        scratch_shapes=[
            pltpu.VMEM((hw, 2 * ta), BF16),
            pltpu.VMEM((1, 2 * ta), F32),
            pltpu.VMEM((1, 2 * ta), F32),
            pltpu.VMEM((hw, 2 * ta), F32),
            pltpu.VMEM((n_comp, DA_HD, LANES), F32),
            pltpu.VMEM((PAGE_SIZE, PAGE_SIZE * DA_HEADS), BF16),
            pltpu.VMEM((n_comp, 1), F32),
            pltpu.VMEM((n_comp, 1), F32),
            pltpu.VMEM((n_comp, hw), F32),
        ],
    )
    return pl.pallas_call(
        functools.partial(_attention_fused_body, nq, n_groups, npg),
        grid_spec=grid_spec,
        out_shape=[jax.ShapeDtypeStruct((n_seq * seq, d), BF16),
                   jax.ShapeDtypeStruct((n, DA_HEADS, hw), F32)],
        compiler_params=_params(1),
        name="attention_fused",
    )(page_table.reshape(-1), *lams, qt, kb, vt, subln_col, row3(q), row3(k_new), row3(v_new),
      *([cache_kt] * npg), *([cache_v] * npg), subln_row)


def _cumsum_rows(g, tri):
    hi = g.astype(BF16)
    r1 = g - hi.astype(F32)
    mid = r1.astype(BF16)
    lo = (r1 - mid.astype(F32)).astype(BF16)
    return _mm(tri, hi) + _mm(tri, mid) + _mm(tri, lo)


def _recurrence_tile(q, k, v, g, st_ref, n_heads, dk, dv, c, post):
    t = q.shape[0]
    nc = t // c
    row = lax.broadcasted_iota(jnp.int32, (t, t), 0)
    col = lax.broadcasted_iota(jnp.int32, (t, t), 1)
    same_chunk = row // c == col // c
    tri = jnp.where(same_chunk & (row >= col), 1.0, 0.0).astype(BF16)
    all_g = _cumsum_rows(g, tri)
    q_state, q_dec, k_dec, k_tail, decay, vb = [], [], [], [], [], []
    for ci in range(nc):
        rows = slice(ci * c, (ci + 1) * c)
        big_g = all_g[rows]
        g_mid = all_g[ci * c + c // 2 - 1:ci * c + c // 2, :]
        g_last = all_g[(ci + 1) * c - 1:(ci + 1) * c, :]
        q_state.append((q[rows] * jnp.exp(big_g)).astype(BF16))
        q_dec.append((q[rows] * jnp.exp(big_g - g_mid)).astype(BF16))
        k_dec.append((k[rows] * jnp.exp(g_mid - big_g)).astype(BF16))
        k_tail.append((k[rows] * jnp.exp(g_last - big_g)).astype(BF16))
        decay.append(jnp.exp(g_last))
        vb.append(v[rows].astype(BF16))
    causal = (lax.broadcasted_iota(jnp.int32, (c, c), 0)
              >= lax.broadcasted_iota(jnp.int32, (c, c), 1))
    pairs = [(ci, h) for ci in range(nc) for h in range(n_heads)]
    ks = lambda h: slice(h * dk, (h + 1) * dk)
    vs = lambda h: slice(h * dv, (h + 1) * dv)
    scores = {p: _mm_nt(q_dec[p[0]][:, ks(p[1])], k_dec[p[0]][:, ks(p[1])]) for p in pairs}
    update = {p: _mm_tn(vb[p[0]][:, vs(p[1])], k_tail[p[0]][:, ks(p[1])]) for p in pairs}
    scores = {p: jnp.where(causal, s, 0.0).astype(BF16) for p, s in scores.items()}
    intra = {p: _mm(scores[p], vb[p[0]][:, vs(p[1])]) for p in pairs}
    entering = {}
    for h in range(n_heads):
        state = st_ref[h]
        for ci in range(nc):
            entering[ci, h] = state.astype(BF16)
            state = state * decay[ci][:, ks(h)] + update[ci, h]
        st_ref[h] = state
    inter = {p: _mm_nt(q_state[p[0]][:, ks(p[1])], entering[p]) for p in pairs}
    return jnp.concatenate(
        [jnp.concatenate([post(intra[ci, h] + inter[ci, h], h, ci) for h in range(n_heads)], axis=1)
         for ci in range(nc)], axis=0)


def _hgrn_lower_bound(lbl_ref, layer):
    logits = lbl_ref[...]
    e = jnp.exp(logits - jnp.max(logits, axis=0, keepdims=True))
    p = e / jnp.sum(e, axis=0, keepdims=True)
    return jnp.sum(p[0:layer + 1, :], axis=0, keepdims=True) - p[0:1, :]


def _hgrn_gates(p, lb, d):
    forget = lb + (1.0 - lb) * jax.nn.sigmoid(p[:, d:2 * d])
    return _silu(p[:, :d]), 1.0 - forget, p[:, 2 * d:], jnp.log(forget)


def _gla_gates(p, gk, wgk2_ref, bgk2_ref, d):
    key = d // 2
    z = _mm(gk.astype(BF16), wgk2_ref[...]) + bgk2_ref[...]
    log_sig = jnp.minimum(z, 0.0) - jnp.log(1.0 + jnp.exp(-jnp.abs(z)))
    q = p[:, :key] * ((key // GLA_HEADS) ** -0.5)
    return q, p[:, key:2 * key], p[:, 2 * key:2 * key + d], log_sig / GLA_GATE_NORM


def _recurrent_prompt_body(kind, layer, *refs):
    if kind == "hgrn":
        (x_ref, win_ref, lbl_ref, ng_ref, wout_ref, g_ref, b_ref,
         o_ref, sfin_ref, st_ref) = refs
    else:
        (x_ref, win_ref, wgk_ref, wgk2_ref, bgk2_ref, ng_ref, wout_ref, g_ref, b_ref,
         o_ref, sfin_ref, st_ref) = refs
    t = pl.program_id(1)
    tm, d = x_ref.shape
    n_heads, dv, dk = st_ref.shape

    @pl.when(t == 0)
    def _reset():
        st_ref[...] = jnp.zeros(st_ref.shape, F32)

    x = x_ref[...]
    xb = x.astype(BF16)
    p = _mm(xb, win_ref[...])
    ng = ng_ref[...]
    chunk = min(REC_CHUNK, tm)
    if kind == "hgrn":
        q, k, v, g = _hgrn_gates(p, _hgrn_lower_bound(lbl_ref, layer), d)
        post = lambda o, h, ci: _rms_norm(o, ng)
    else:
        q, k, v, g = _gla_gates(p, _mm(xb, wgk_ref[...]), wgk2_ref, bgk2_ref, d)
        gate = _silu(p[:, 2 * d:3 * d])
        post = lambda o, h, ci: (_rms_norm(o, ng)
                                 * gate[ci * chunk:(ci + 1) * chunk, h * dv:(h + 1) * dv])
    on = _recurrence_tile(q, k, v, g, st_ref, n_heads, dk, dv, chunk, post)
    y = _mm(on.astype(BF16), wout_ref[...])
    o_ref[...] = _layer_norm(ALPHA * x + y, g_ref[...], b_ref[...])

    @pl.when(t == pl.num_programs(1) - 1)
    def _state():
        for h in range(n_heads):
            sfin_ref[0, h] = st_ref[h].T


def _recurrent_prompt(kind, layer, x, n_seq, n_heads, dk, dv, weights, w_out, g, b):
    t, d = x.shape
    seq = t // n_seq
    tm = min(MIX_TILE, seq)
    nt = seq // tm
    row = lambda s, i: (s * nt + i, 0)
    w_specs = [_const_spec(w.shape) for w in weights]
    return pl.pallas_call(
        functools.partial(_recurrent_prompt_body, kind, layer),
        grid=(n_seq, nt),
        in_specs=[pl.BlockSpec((tm, d), row)] + w_specs + [
            _const_spec((d, d)), _const_spec((1, d)), _const_spec((1, d))],
        out_specs=[
            pl.BlockSpec((tm, d), row),
            pl.BlockSpec((1, n_heads, dk, dv), lambda s, i: (s, 0, 0, 0)),
        ],
        out_shape=[
            jax.ShapeDtypeStruct((t, d), F32),
            jax.ShapeDtypeStruct((n_seq, n_heads, dk, dv), F32),
        ],
        scratch_shapes=[pltpu.VMEM((n_heads, dv, dk), F32)],
        compiler_params=_params(2),
        name=kind + "_prompt",
    )(x, *weights, w_out, g, b)


def _recurrent_sample_body(kind, layer, *refs):
    if kind == "hgrn":
        (x_ref, s_ref, win_ref, lbl_ref, ng_ref, wout_ref, g_ref, b_ref,
         o_ref, snew_ref, q_scr, k_scr, v_scr, f_scr, gate_scr, on_scr) = refs
    else:
        (x_ref, s_ref, win_ref, wgk_ref, wgk2_ref, bgk2_ref, ng_ref, wout_ref, g_ref, b_ref,
         o_ref, snew_ref, q_scr, k_scr, v_scr, f_scr, gate_scr, on_scr) = refs
    n = pl.program_id(0)
    d = x_ref.shape[1]
    _, n_heads, dk, dv = s_ref.shape

    @pl.when(n == 0)
    def _project():
        xb = x_ref[...].astype(BF16)
        p = _mm(xb, win_ref[...])
        if kind == "hgrn":
            q, k, v, g = _hgrn_gates(p, _hgrn_lower_bound(lbl_ref, layer), d)
            gate_scr[...] = jnp.zeros(gate_scr.shape, F32)
        else:
            q, k, v, g = _gla_gates(p, _mm(xb, wgk_ref[...]), wgk2_ref, bgk2_ref, d)
            gate_scr[...] = p[:, 2 * d:3 * d]
        q_scr[...] = q
        k_scr[...] = k
        v_scr[...] = v
        f_scr[...] = jnp.exp(g)

    eye = _identity_bf16()
    q = q_scr[pl.ds(n, 1), :]
    k = k_scr[pl.ds(n, 1), :]
    v = v_scr[pl.ds(n, 1), :]
    f = f_scr[pl.ds(n, 1), :]
    f_hi = f.astype(BF16)
    f_r = f - f_hi.astype(F32)
    f_mid = f_r.astype(BF16)
    f_lo = (f_r - f_mid.astype(F32)).astype(BF16)
    f_col = (_lane_broadcast_columns(f_hi, eye) + _lane_broadcast_columns(f_mid, eye)
             + _lane_broadcast_columns(f_lo, eye))
    k_col = _lane_broadcast_columns(k.astype(BF16), eye)
    reps = dv // LANES
    widen = lambda a: a if reps == 1 else jnp.concatenate([a] * reps, axis=1)
    outs = []
    for h in range(n_heads):
        ks = slice(h * dk, (h + 1) * dk)
        vs = slice(h * dv, (h + 1) * dv)
        s_new = widen(f_col[ks]) * s_ref[0, h] + widen(k_col[ks]) * v[:, vs]
        snew_ref[0, h] = s_new
        q_rows = jnp.broadcast_to(q[:, ks], (2 * SUBLANES, dk)).astype(BF16)
        outs.append(_mm(q_rows, s_new.astype(BF16))[0:1, :])
    on_scr[pl.ds(n, 1), :] = jnp.concatenate(outs, axis=1)

    @pl.when(n == pl.num_programs(0) - 1)
    def _finish():
        ng = ng_ref[...]
        o = on_scr[...]
        parts = []
        for h in range(n_heads):
            oh = _rms_norm(o[:, h * dv:(h + 1) * dv], ng)
            if kind == "gla":
                oh = oh * _silu(gate_scr[:, h * dv:(h + 1) * dv])
            parts.append(oh)
        y = _mm(jnp.concatenate(parts, axis=1).astype(BF16), wout_ref[...])
        o_ref[...] = _layer_norm(ALPHA * x_ref[...] + y, g_ref[...], b_ref[...])


def _recurrent_sample(kind, layer, x, state, weights, w_out, g, b):
    n, d = x.shape
    _, n_heads, dk, dv = state.shape
    w_specs = [_const_spec(w.shape) for w in weights]
    st_spec = pl.BlockSpec((1, n_heads, dk, dv), lambda s: (s, 0, 0, 0))
    return pl.pallas_call(
        functools.partial(_recurrent_sample_body, kind, layer),
        grid=(n,),
        in_specs=[_const_spec((n, d)), st_spec] + w_specs + [
            _const_spec((d, d)), _const_spec((1, d)), _const_spec((1, d))],
        out_specs=[_const_spec((n, d)), st_spec],
        out_shape=[jax.ShapeDtypeStruct((n, d), F32), jax.ShapeDtypeStruct(state.shape, F32)],
        scratch_shapes=[
            pltpu.VMEM((n, n_heads * dk), F32),
            pltpu.VMEM((n, n_heads * dk), F32),
            pltpu.VMEM((n, n_heads * dv), F32),
            pltpu.VMEM((n, n_heads * dk), F32),
            pltpu.VMEM((n, d), F32),
            pltpu.VMEM((n, n_heads * dv), F32),
        ],
        compiler_params=_params(1),
        name=kind + "_sample",
    )(x, state, *weights, w_out, g, b)


def kernel(x_prompt, x_sample, state_conv, cache_k, cache_v, page_table, state_hgrn, state_gla,
           ffn_w_up, ffn_w_down, ln_g, ln_b,
           conv_w_in, conv_w, conv_w_out,
           da_w_in, da_lambda_q1, da_lambda_k1, da_lambda_q2, da_lambda_k2, da_subln_g, da_w_out,
           hg_w_in, hg_lb_logits, hg_norm_g, hg_w_out,
           gla_w_in, gla_w_gk2, gla_b_gk2, gla_norm_g, gla_w_out):
    n_seq, seq, d = x_prompt.shape
    n_smp = x_sample.shape[0]
    xp = x_prompt.reshape(n_seq * seq, d)
    xs = x_sample.reshape(n_smp, d)
    bf = lambda w: w.astype(BF16)
    row = lambda v: v.reshape(1, -1)
    w_up, w_down = bf(ffn_w_up), bf(ffn_w_down)

    def ffn_pair(i, half, xp, xs):
        g, b = row(ln_g[i, half * 2]), row(ln_b[i, half * 2])
        return _ffn(xp, xs, w_up, w_down, i, half, g, b)

    xp, xs = ffn_pair(0, 0, xp, xs)
    g, b = row(ln_g[0, 1]), row(ln_b[0, 1])
    cw_in, cw_out = bf(conv_w_in), bf(conv_w_out)
    xp, conv_p = _conv_prompt(xp, n_seq, cw_in, conv_w, cw_out, g, b)
    xs, u_s = _conv_sample(xs, state_conv[:, 0], state_conv[:, 1], cw_in, conv_w, cw_out, g, b)
    conv_s = jnp.stack([state_conv[:, 1], u_s], axis=1)
    xp, xs = ffn_pair(0, 1, xp, xs)

    xp, xs = ffn_pair(1, 0, xp, xs)
    g, b = row(ln_g[1, 1]), row(ln_b[1, 1])
    lams = [row(v) for v in (da_lambda_q1, da_lambda_k1, da_lambda_q2, da_lambda_k2)]
    subln = row(da_subln_g)
    dw_in, dw_out = bf(da_w_in), bf(da_w_out)
    past = page_table.shape[1] * PAGE_SIZE
    qt, kt, kb, v, vt = _da_qkv_prompt(
        xp, n_seq, dw_in[:, :d].T, dw_in[:, d:2 * d].T, dw_in[:, 2 * d:],
        _rope_tables(jnp.arange(seq, dtype=jnp.int32)))
    qs, ks, vs = _da_qkv_sample(xs, dw_in, _rope_tables(jnp.full((1,), past, jnp.int32)))
    subln_col = da_subln_g.reshape(-1, 1)
    cache_kt = cache_k.transpose(0, 2, 3, 1)
    flash_steps = n_seq * DA_HEADS * vt.shape[1]
    decode_steps = n_smp * (page_table.shape[1] // math.gcd(DECODE_PAGES_PER_STEP, page_table.shape[1]))
    if flash_steps == decode_steps:
        on, ons = _attention_fused(qt, kb, vt, qs, ks, vs, cache_kt, cache_v, page_table, lams,
                                   subln_col, subln)
    else:
        on = _flash_prompt(qt, kb, vt, lams, subln_col)
        ons = _decode_attn(qs, ks, vs, cache_kt, cache_v, page_table, lams, subln)
    k_p = kt.reshape(n_seq, 2 * DA_HEADS, DA_HD, seq).transpose(0, 3, 1, 2)
    v_p = v.reshape(n_seq, seq, DA_HEADS, 2 * DA_HD)
    xp = _proj_norm(on, xp, dw_out, g, b)
    k_s = ks.reshape(n_smp, 1, 2 * DA_HEADS, DA_HD)
    v_s = vs.reshape(n_smp, 1, DA_HEADS, 2 * DA_HD)
    xs = _proj_norm(ons.reshape(n_smp, d).astype(BF16), xs, dw_out, g, b)
    xp, xs = ffn_pair(1, 1, xp, xs)

    xp, xs = ffn_pair(2, 0, xp, xs)
    g, b = row(ln_g[2, 1]), row(ln_b[2, 1])
    hg_weights = (bf(hg_w_in), hg_lb_logits, row(hg_norm_g))
    dk = d // HG_HEADS
    xp, hg_p = _recurrent_prompt("hgrn", 2, xp, n_seq, HG_HEADS, dk, dk, hg_weights,
                                 bf(hg_w_out), g, b)
    xs, hg_s = _recurrent_sample("hgrn", 2, xs, state_hgrn, hg_weights, bf(hg_w_out), g, b)
    xp, xs = ffn_pair(2, 1, xp, xs)

    xp, xs = ffn_pair(3, 0, xp, xs)
    g, b = row(ln_g[3, 1]), row(ln_b[3, 1])
    main = 3 * d
    w_gk = jnp.pad(bf(gla_w_in[:, main:]), ((0, 0), (0, LANES - GLA_LOWRANK)))
    w_gk2 = jnp.pad(bf(gla_w_gk2), ((0, LANES - GLA_LOWRANK), (0, 0)))
    gla_weights = (bf(gla_w_in[:, :main]), w_gk, w_gk2, row(gla_b_gk2), row(gla_norm_g))
    gdk = d // 2 // GLA_HEADS
    gdv = d // GLA_HEADS
    xp, gla_p = _recurrent_prompt("gla", 3, xp, n_seq, GLA_HEADS, gdk, gdv, gla_weights,
                                  bf(gla_w_out), g, b)
    xs, gla_s = _recurrent_sample("gla", 3, xs, state_gla, gla_weights, bf(gla_w_out), g, b)
    xp, xs = ffn_pair(3, 1, xp, xs)

    return (xp.reshape(n_seq, seq, d), xs.reshape(n_smp, 1, d), conv_p, conv_s,
            k_p, v_p, k_s, v_s, hg_p, hg_s, gla_p, gla_s)
```

```python
import functools
import math

import jax
import jax.numpy as jnp
from jax import lax
from jax.experimental import pallas as pl
from jax.experimental.pallas import tpu as pltpu

F32 = jnp.float32
BF16 = jnp.bfloat16

DEPTH = 4
LN_EPS = 1e-5
RMS_EPS = 1e-6
ALPHA = (2.0 * DEPTH) ** 0.25
CONV_W = 3
DA_HEADS = 8
DA_HD = 64
ROT_DIM = DA_HD // 4
ROPE_THETA = 500000.0
DA_LAMBDA_INIT = 0.8 - 0.6 * math.exp(-0.3 * 1)
NEG_INF = -1e30
SCORE_SCALE = DA_HD ** -0.5 * math.log2(math.e)
PAGE_SIZE = 128
HG_HEADS = 8
GLA_HEADS = 4
GLA_LOWRANK = 16
GLA_GATE_NORM = 16.0

LANES = 128
SUBLANES = 8
MXU_DIM = 256
VMEM_LIMIT = 56 * 1024 * 1024

TOKEN_TILE = 512
FFN_TILE = 512
FFN_CHUNK = 1408
MIX_TILE = 256
REC_CHUNK = 64
ATT_TILE = 512
DECODE_PAGES_PER_STEP = 8


def _mm(a, b):
    return jnp.dot(a, b, preferred_element_type=F32)


def _mm_nt(a, b):
    return lax.dot_general(a, b, (((1,), (1,)), ((), ())), preferred_element_type=F32)


def _mm_tn(a, b):
    return lax.dot_general(a, b, (((0,), (0,)), ((), ())), preferred_element_type=F32)


def _layer_norm(z, g, b):
    mu = jnp.mean(z, axis=-1, keepdims=True)
    zc = z - mu
    var = jnp.mean(zc * zc, axis=-1, keepdims=True)
    return zc * lax.rsqrt(var + LN_EPS) * g + b


def _rms_norm(o, g):
    return o * lax.rsqrt(jnp.mean(o * o, axis=-1, keepdims=True) + RMS_EPS) * g


def _silu(a):
    return a * jax.nn.sigmoid(a)


def _params(n_axes):
    return pltpu.CompilerParams(dimension_semantics=("arbitrary",) * n_axes,
                                vmem_limit_bytes=VMEM_LIMIT)


def _const_spec(shape):
    nd = len(shape)
    return pl.BlockSpec(shape, lambda *_: (0,) * nd)


def _ffn_chunks(f):
    n = max(1, f // FFN_CHUNK)
    edges = [MXU_DIM * round(i * f / n / MXU_DIM) for i in range(n)] + [f]
    return list(zip(edges[:-1], edges[1:]))


def _ffn_body(nt, x_ref, xs_ref, wu_ref, wd_ref, g_ref, b_ref, o_ref, os_ref, z_ref):
    i = pl.program_id(0)
    f = wd_ref.shape[0]
    g = g_ref[...]
    b = b_ref[...]

    def residual_plus_ffn(x, norm_of=None):
        xb = x.astype(BF16)
        acc = None
        normed = None
        for c0, c1 in _ffn_chunks(f):
            a = _mm(xb, wu_ref[:, c0:c1])
            u = _mm(xb, wu_ref[:, f + c0:f + c1])
            if norm_of is not None and normed is None:
                normed = _layer_norm(norm_of, g, b)
            h = (_silu(a) * u).astype(BF16)
            part = _mm(h, wd_ref[c0:c1, :])
            acc = part if acc is None else acc + part
        return ALPHA * x + 0.5 * acc, normed

    @pl.when(i == 0)
    def _init():
        z_ref[...] = jnp.zeros(z_ref.shape, F32)

    @pl.when(i < nt)
    def _tile():
        z_new, normed = residual_plus_ffn(x_ref[...], z_ref[...])
        o_ref[...] = normed
        z_ref[...] = z_new

    @pl.when(i == nt)
    def _tail():
        o_ref[...] = _layer_norm(z_ref[...], g, b)
        zs, _ = residual_plus_ffn(xs_ref[...])
        os_ref[...] = _layer_norm(zs, g, b)


def _ffn(x, xs, w_up, w_down, layer, half, g, b):
    t, d = x.shape
    ns = xs.shape[0]
    f = w_down.shape[2]
    tm = min(FFN_TILE, t)
    nt = t // tm
    once = pl.Buffered(1)
    return pl.pallas_call(
        functools.partial(_ffn_body, nt),
        grid=(nt + 1,),
        in_specs=[
            pl.BlockSpec((tm, d), lambda i: (jnp.minimum(i, nt - 1), 0)),
            _const_spec((ns, d)),
            pl.BlockSpec((None, None, d, 2 * f), lambda i: (layer, half, 0, 0), pipeline_mode=once),
            pl.BlockSpec((None, None, f, d), lambda i: (layer, half, 0, 0), pipeline_mode=once),
            _const_spec((1, d)),
            _const_spec((1, d)),
        ],
        out_specs=[
            pl.BlockSpec((tm, d), lambda i: (jnp.maximum(i - 1, 0), 0)),
            _const_spec((ns, d)),
        ],
        out_shape=[jax.ShapeDtypeStruct((t, d), F32), jax.ShapeDtypeStruct((ns, d), F32)],
        scratch_shapes=[pltpu.VMEM((tm, d), F32)],
        compiler_params=_params(1),
        name="ffn",
    )(x, xs, w_up, w_down, g, b)


def _proj_norm_body(y_ref, x_ref, w_ref, g_ref, b_ref, o_ref):
    y = _mm(y_ref[...], w_ref[...])
    o_ref[...] = _layer_norm(ALPHA * x_ref[...] + y, g_ref[...], b_ref[...])


def _proj_norm(y, x, w, g, b):
    t, d = x.shape
    tm = min(TOKEN_TILE, t)
    return pl.pallas_call(
        _proj_norm_body,
        grid=(t // tm,),
        in_specs=[
            pl.BlockSpec((tm, d), lambda i: (i, 0)),
            pl.BlockSpec((tm, d), lambda i: (i, 0)),
            _const_spec((d, d)),
            _const_spec((1, d)),
            _const_spec((1, d)),
        ],
        out_specs=pl.BlockSpec((tm, d), lambda i: (i, 0)),
        out_shape=jax.ShapeDtypeStruct((t, d), F32),
        compiler_params=_params(1),
        name="proj_norm",
    )(y, x, w, g, b)


def _conv_prompt_body(x_ref, win_ref, cw_ref, wout_ref, g_ref, b_ref, o_ref, st_ref, ubuf_ref):
    t = pl.program_id(1)
    tm, d = x_ref.shape

    @pl.when(t == 0)
    def _reset():
        ubuf_ref[0:SUBLANES, :] = jnp.zeros((SUBLANES, d), F32)

    x = x_ref[...]
    p = _mm(x.astype(BF16), win_ref[...])
    u = p[:, d:2 * d] * p[:, 2 * d:]
    ubuf_ref[SUBLANES:, :] = u
    ue = ubuf_ref[...]
    um1 = pltpu.roll(ue, 1, 0)[SUBLANES:, :]
    um2 = pltpu.roll(ue, 2, 0)[SUBLANES:, :]
    cw = cw_ref[...]
    conv = cw[0:1, :] * um2 + cw[1:2, :] * um1 + cw[2:3, :] * u
    y = _mm((p[:, :d] * conv).astype(BF16), wout_ref[...])
    o_ref[...] = _layer_norm(ALPHA * x + y, g_ref[...], b_ref[...])
    ubuf_ref[0:SUBLANES, :] = ubuf_ref[tm:tm + SUBLANES, :]

    @pl.when(t == pl.num_programs(1) - 1)
    def _state():
        st_ref[0] = ubuf_ref[tm + SUBLANES - (CONV_W - 1):tm + SUBLANES, :]


def _conv_prompt(x, n_seq, w_in, conv_w, w_out, g, b):
    t, d = x.shape
    seq = t // n_seq
    tm = min(TOKEN_TILE, seq)
    nt = seq // tm
    return pl.pallas_call(
        _conv_prompt_body,
        grid=(n_seq, nt),
        in_specs=[
            pl.BlockSpec((tm, d), lambda s, i: (s * nt + i, 0)),
            _const_spec((d, 3 * d)),
            _const_spec((CONV_W, d)),
            _const_spec((d, d)),
            _const_spec((1, d)),
            _const_spec((1, d)),
        ],
        out_specs=[
            pl.BlockSpec((tm, d), lambda s, i: (s * nt + i, 0)),
            pl.BlockSpec((1, CONV_W - 1, d), lambda s, i: (s, 0, 0)),
        ],
        out_shape=[
            jax.ShapeDtypeStruct((t, d), F32),
            jax.ShapeDtypeStruct((n_seq, CONV_W - 1, d), F32),
        ],
        scratch_shapes=[pltpu.VMEM((tm + SUBLANES, d), F32)],
        compiler_params=_params(2),
        name="conv_prompt",
    )(x, w_in, conv_w, w_out, g, b)


def _conv_sample_body(x_ref, s0_ref, s1_ref, win_ref, cw_ref, wout_ref, g_ref, b_ref,
                      o_ref, u_ref):
    d = x_ref.shape[1]
    x = x_ref[...]
    p = _mm(x.astype(BF16), win_ref[...])
    u = p[:, d:2 * d] * p[:, 2 * d:]
    cw = cw_ref[...]
    conv = cw[0:1, :] * s0_ref[...] + cw[1:2, :] * s1_ref[...] + cw[2:3, :] * u
    y = _mm((p[:, :d] * conv).astype(BF16), wout_ref[...])
    o_ref[...] = _layer_norm(ALPHA * x + y, g_ref[...], b_ref[...])
    u_ref[...] = u


def _conv_sample(x, s0, s1, w_in, conv_w, w_out, g, b):
    n, d = x.shape
    return pl.pallas_call(
        _conv_sample_body,
        out_shape=[jax.ShapeDtypeStruct((n, d), F32), jax.ShapeDtypeStruct((n, d), F32)],
        compiler_params=pltpu.CompilerParams(vmem_limit_bytes=VMEM_LIMIT),
        name="conv_sample",
    )(x, s0, s1, w_in, conv_w, w_out, g, b)


def _rope_tables(pos):
    inv_freq = 1.0 / (ROPE_THETA ** (jnp.arange(0, ROT_DIM, 2, dtype=F32) / ROT_DIM))
    ang = pos.astype(F32)[:, None] * inv_freq[None, :]
    cos, sin = jnp.cos(ang), jnp.sin(ang)
    half = ROT_DIM // 2
    n = pos.shape[0]
    rest = DA_HD - ROT_DIM
    c64 = jnp.concatenate([cos, cos, jnp.ones((n, rest), F32)], axis=1)
    up64 = jnp.concatenate([-sin, jnp.zeros((n, DA_HD - half), F32)], axis=1)
    dn64 = jnp.concatenate([jnp.zeros((n, half), F32), sin, jnp.zeros((n, rest), F32)], axis=1)
    rep = LANES // DA_HD
    return (jnp.tile(c64, (1, rep)), jnp.tile(up64, (1, rep)), jnp.tile(dn64, (1, rep)),
            cos.T, sin.T)


def _rope_rows(x, c, s_up, s_dn):
    half = ROT_DIM // 2
    blocks = []
    for i in range(x.shape[1] // LANES):
        xb = x[:, i * LANES:(i + 1) * LANES]
        blocks.append(xb * c + pltpu.roll(xb, LANES - half, 1) * s_up + pltpu.roll(xb, half, 1) * s_dn)
    return jnp.concatenate(blocks, axis=1)


def _rope_cols(xt, ct, st):
    d, cols = xt.shape
    half = ROT_DIM // 2
    x3 = xt.reshape(d // DA_HD, DA_HD, cols)
    x1 = x3[:, 0:half, :]
    x2 = x3[:, half:ROT_DIM, :]
    x3 = jnp.concatenate([x1 * ct - x2 * st, x2 * ct + x1 * st, x3[:, ROT_DIM:, :]], axis=1)
    return x3.reshape(d, cols)


def _da_qkv_prompt_body(x_ref, wqt_ref, wkt_ref, wv_ref, ct_ref, st_ref,
                        qt_ref, kt_ref, kb_ref, v_ref, vt_ref):
    xb = x_ref[...].astype(BF16)
    ct = ct_ref[...]
    st = st_ref[...]
    qt = _rope_cols(_mm_nt(wqt_ref[...], xb), ct, st)
    qt_ref[0] = (qt * SCORE_SCALE).astype(BF16)
    kt = _rope_cols(_mm_nt(wkt_ref[...], xb), ct, st)
    kt_ref[0] = kt
    kb_ref[...] = kt.T.astype(BF16)
    v = _mm(xb, wv_ref[...])
    v_ref[...] = v
    vt_ref[0, 0] = v.T.astype(BF16)


def _da_qkv_prompt(x, n_seq, wqt, wkt, wv, tables):
    t, d = x.shape
    seq = t // n_seq
    tm = min(ATT_TILE, seq)
    nt = seq // tm
    ct, st = tables[3:]
    half = ROT_DIM // 2
    row = lambda s, i: (s * nt + i, 0)
    return pl.pallas_call(
        _da_qkv_prompt_body,
        grid=(n_seq, nt),
        in_specs=[
            pl.BlockSpec((tm, d), row),
            _const_spec((d, d)),
            _const_spec((d, d)),
            _const_spec((d, d)),
            pl.BlockSpec((half, tm), lambda s, i: (0, i)),
            pl.BlockSpec((half, tm), lambda s, i: (0, i)),
        ],
        out_specs=[
            pl.BlockSpec((1, d, tm), lambda s, i: (s, 0, i)),
            pl.BlockSpec((1, d, tm), lambda s, i: (s, 0, i)),
            pl.BlockSpec((tm, d), row),
            pl.BlockSpec((tm, d), row),
            pl.BlockSpec((1, 1, d, tm), lambda s, i: (s, i, 0, 0)),
        ],
        out_shape=[
            jax.ShapeDtypeStruct((n_seq, d, seq), BF16),
            jax.ShapeDtypeStruct((n_seq, d, seq), F32),
            jax.ShapeDtypeStruct((t, d), BF16),
            jax.ShapeDtypeStruct((t, d), F32),
            jax.ShapeDtypeStruct((n_seq, nt, d, tm), BF16),
        ],
        compiler_params=_params(2),
        name="da_qkv_prompt",
    )(x, wqt, wkt, wv, ct, st)


def _diff_lambda(lq1_ref, lk1_ref, lq2_ref, lk2_ref):
    s1 = jnp.sum(lq1_ref[...] * lk1_ref[...], axis=-1, keepdims=True)
    s2 = jnp.sum(lq2_ref[...] * lk2_ref[...], axis=-1, keepdims=True)
    return jnp.exp(s1) - jnp.exp(s2) + DA_LAMBDA_INIT


def _flash_body(lq1_ref, lk1_ref, lq2_ref, lk2_ref, *refs):
    _flash_step(pl.program_id(2), (lq1_ref, lk1_ref, lq2_ref, lk2_ref), *refs)


def _flash_step(qi, lam_refs, qt_ref, k_ref, vt_ref, sg_ref, o_ref, q2_ref, m_ref, l_ref, acc_ref):
    tq = qt_ref.shape[2]
    tk = vt_ref.shape[3]
    n_col = 2 * tq
    cb = min(2 * LANES, tq)

    qt = qt_ref[0]
    sub = lax.broadcasted_iota(jnp.int32, qt.shape, 0)
    zero = jnp.zeros_like(qt)
    q2_ref[:, 0:tq] = jnp.where(sub < DA_HD, qt, zero)
    q2_ref[:, tq:] = jnp.where(sub >= DA_HD, qt, zero)
    m_ref[...] = jnp.full(m_ref.shape, NEG_INF, F32)
    l_ref[...] = jnp.zeros(l_ref.shape, F32)
    acc_ref[...] = jnp.zeros(acc_ref.shape, F32)

    starts = list(range(0, n_col, cb))

    def kv_tiles(kjs, diagonal):
        k_tiles = [k_ref[pl.ds(pl.multiple_of(kj * tk, tk), tk), :] for kj in kjs]
        vt_tiles = [vt_ref[0, kj] for kj in kjs]
        items = [(t, c0) for t in range(len(kjs)) for c0 in starts]
        n_items = len(items)

        def score(n):
            t, c0 = items[n]
            return _mm(k_tiles[t], q2_ref[:, c0:c0 + cb])

        def max_pass(n, st):
            c0 = items[n][1]
            if diagonal:
                key = lax.broadcasted_iota(jnp.int32, st.shape, 0)
                qry = lax.broadcasted_iota(jnp.int32, st.shape, 1) + (c0 % tq)
                st = jnp.where(key <= qry, st, NEG_INF)
            m_prev = m_ref[:, c0:c0 + cb]
            return st, m_prev, jnp.maximum(m_prev, jnp.max(st, axis=0, keepdims=True))

        def exp_pass(n, st, m_prev, m_new):
            t, c0 = items[n]
            cs = slice(c0, c0 + cb)
            alpha = jnp.exp2(m_prev - m_new)
            pt = jnp.exp2(st - m_new)
            l_ref[:, cs] = alpha * l_ref[:, cs] + jnp.sum(pt, axis=0, keepdims=True)
            acc_ref[:, cs] = alpha * acc_ref[:, cs] + _mm(vt_tiles[t], pt.astype(BF16))
            m_ref[:, cs] = m_new

        raw = {n: score(n) for n in range(min(2, n_items))}
        ready = {0: max_pass(0, raw.pop(0))}
        for n in range(n_items):
            if n + 2 < n_items:
                raw[n + 2] = score(n + 2)
            if n + 1 < n_items:
                ready[n + 1] = max_pass(n + 1, raw.pop(n + 1))
            exp_pass(n, *ready.pop(n))

    def tile_pair(jj, carry):
        kv_tiles([2 * jj, 2 * jj + 1], False)
        return carry

    lax.fori_loop(0, lax.shift_right_logical(qi, 1), tile_pair, 0)

    @pl.when(jnp.bitwise_and(qi, 1) == 1)
    def _odd_tile():
        kv_tiles([qi - 1], False)

    kv_tiles([qi], True)

    lam = _diff_lambda(*lam_refs)
    o = acc_ref[...] / l_ref[...]
    o = o[:, 0:tq] - lam * o[:, tq:]
    r = lax.rsqrt(jnp.mean(o * o, axis=0, keepdims=True) + RMS_EPS)
    o = o * r * sg_ref[...] * (1.0 - DA_LAMBDA_INIT)
    o_ref[...] = o.T.astype(BF16)


def _flash_prompt(qt, kb, vt, lams, subln_col):
    n_seq, d, seq = qt.shape
    nq, ta = vt.shape[1], vt.shape[3]
    hw = 2 * DA_HD
    return pl.pallas_call(
        _flash_body,
        grid=(n_seq, DA_HEADS, nq),
        in_specs=[_const_spec((1, DA_HD))] * 4 + [
            pl.BlockSpec((1, hw, ta), lambda s, h, i: (s, h, i)),
            pl.BlockSpec((seq, hw), lambda s, h, i: (s, h)),
            pl.BlockSpec((1, nq, hw, ta), lambda s, h, i: (s, 0, h, 0)),
            _const_spec((hw, 1)),
        ],
        out_specs=pl.BlockSpec((ta, hw), lambda s, h, i: (s * nq + i, h)),
        out_shape=jax.ShapeDtypeStruct((n_seq * seq, d), BF16),
        scratch_shapes=[
            pltpu.VMEM((hw, 2 * ta), BF16),
            pltpu.VMEM((1, 2 * ta), F32),
            pltpu.VMEM((1, 2 * ta), F32),
            pltpu.VMEM((hw, 2 * ta), F32),
        ],
        compiler_params=_params(3),
        name="flash_prompt",
    )(*lams, qt, kb, vt, subln_col)


def _da_qkv_sample_body(x_ref, w_ref, c_ref, up_ref, dn_ref, q_ref, k_ref, v_ref):
    d = x_ref.shape[1]
    p = _mm(x_ref[...].astype(BF16), w_ref[...])
    c, up, dn = c_ref[...], up_ref[...], dn_ref[...]
    q_ref[...] = _rope_rows(p[:, :d], c, up, dn) * SCORE_SCALE
    k_ref[...] = _rope_rows(p[:, d:2 * d], c, up, dn)
    v_ref[...] = p[:, 2 * d:]


def _da_qkv_sample(x, w_in, tables):
    n, d = x.shape
    c, up, dn = tables[:3]
    return pl.pallas_call(
        _da_qkv_sample_body,
        out_shape=[jax.ShapeDtypeStruct((n, d), F32)] * 3,
        compiler_params=pltpu.CompilerParams(vmem_limit_bytes=VMEM_LIMIT),
        name="da_qkv_sample",
    )(x, w_in, c, up, dn)


def _lane_broadcast_columns(vec_bf, eye):
    w = vec_bf.shape[1]
    rows = jnp.broadcast_to(vec_bf, (LANES, w))
    return jnp.concatenate(
        [_mm_nt(eye, rows[:, i * LANES:(i + 1) * LANES]) for i in range(w // LANES)], axis=0)


def _identity_bf16():
    r = lax.broadcasted_iota(jnp.int32, (LANES, LANES), 0)
    c = lax.broadcasted_iota(jnp.int32, (LANES, LANES), 1)
    return jnp.where(r == c, 1.0, 0.0).astype(BF16)


def _decode_attn_body(npg, pt_ref, lq1_ref, lk1_ref, lq2_ref, lk2_ref, q_ref, kn_ref, vn_ref, *rest):
    del pt_ref
    _decode_step(pl.program_id(1), pl.num_programs(1) - 1, (lq1_ref, lk1_ref, lq2_ref, lk2_ref),
                 q_ref, kn_ref, vn_ref, rest[:npg], rest[npg:2 * npg], *rest[2 * npg:])


def _decode_step(j, j_last, lam_refs, q_ref, kn_ref, vn_ref, kt_refs, v_refs,
                 sg_ref, o_ref, qb_ref, ex_ref, m_ref, l_ref, acc_ref):
    npg = len(kt_refs)
    n_comp = 2 * DA_HEADS
    flat = PAGE_SIZE * DA_HEADS

    @pl.when(j == 0)
    def _init():
        qb = _lane_broadcast_columns(q_ref[0].astype(BF16), _identity_bf16())
        qb_ref[...] = qb.reshape(qb_ref.shape)
        key = lax.broadcasted_iota(jnp.int32, (PAGE_SIZE, flat), 0)
        col = lax.broadcasted_iota(jnp.int32, (PAGE_SIZE, flat), 1)
        ex_ref[...] = jnp.where(col // DA_HEADS == key, 1.0, 0.0).astype(BF16)
        m_ref[...] = jnp.full(m_ref.shape, NEG_INF, F32)
        l_ref[...] = jnp.zeros(l_ref.shape, F32)
        acc_ref[...] = jnp.zeros(acc_ref.shape, F32)

    def scores(kt_of_comp):
        parts = []
        for comp in range(2):
            qb = qb_ref[pl.ds(comp, DA_HEADS, stride=2)]
            parts.append(jnp.sum(qb * kt_of_comp(comp), axis=1))
        return jnp.concatenate(parts, axis=0)

    def online_update(s, pv_of_p):
        m_prev = m_ref[...]
        m_new = jnp.maximum(m_prev, jnp.max(s, axis=-1, keepdims=True))
        alpha = jnp.exp2(m_prev - m_new)
        p = jnp.exp2(s - m_new)
        l_ref[...] = alpha * l_ref[...] + jnp.sum(p, axis=-1, keepdims=True)
        acc_ref[...] = alpha * acc_ref[...] + pv_of_p(p)
        m_ref[...] = m_new

    def pages_pv(p):
        row = lax.broadcasted_iota(jnp.int32, (n_comp, flat), 0)
        col = lax.broadcasted_iota(jnp.int32, (n_comp, flat), 1)
        own_head = col % DA_HEADS == row % DA_HEADS
        pb = p.astype(BF16)
        ex = ex_ref[...]
        spread = [_mm(pb[:, g * PAGE_SIZE:(g + 1) * PAGE_SIZE], ex) for g in range(npg)]
        spread = [jnp.where(own_head, s, 0.0).astype(BF16) for s in spread]
        parts = [_mm(spread[g], v_refs[g][0].reshape(flat, LANES).astype(BF16))
                 for g in range(npg)]
        return functools.reduce(lambda a, b: a + b, parts)

    def page_scores():
        rows = []
        for r in range(n_comp):
            c = 2 * (r % DA_HEADS) + r // DA_HEADS
            qc = qb_ref[c]
            rows.append(jnp.concatenate(
                [jnp.sum(qc * kt[0, c], axis=0, keepdims=True) for kt in kt_refs], axis=1))
        return jnp.concatenate(rows, axis=0)

    online_update(page_scores(), pages_pv)

    @pl.when(j == j_last)
    def _finish():
        kb = _lane_broadcast_columns(kn_ref[0].astype(BF16), _identity_bf16())
        kb = kb.reshape(DA_HEADS, 2, DA_HD, LANES)
        s_self = scores(lambda comp: kb[:, comp])
        lane = lax.broadcasted_iota(jnp.int32, (n_comp, LANES), 1)
        s_self = jnp.where(lane == 0, s_self, NEG_INF)
        vn = vn_ref[0]
        v_rows = jnp.concatenate(
            [vn[:, h * LANES:(h + 1) * LANES] for h in range(DA_HEADS)] * 2, axis=0)
        online_update(s_self, lambda p: jnp.sum(p, axis=-1, keepdims=True) * v_rows)
        lam = _diff_lambda(*lam_refs)
        o = acc_ref[...] / l_ref[...]
        o = o[0:DA_HEADS, :] - lam * o[DA_HEADS:, :]
        o_ref[0] = _rms_norm(o, sg_ref[...]) * (1.0 - DA_LAMBDA_INIT)


def _decode_attn(q, k_new, v_new, cache_kt, cache_v, page_table, lams, subln_g):
    n, d = q.shape
    n_pages = page_table.shape[1]
    npg = math.gcd(DECODE_PAGES_PER_STEP, n_pages)
    n_comp = 2 * DA_HEADS
    hw = 2 * DA_HD
    row3 = lambda a: a.reshape(n, 1, d)
    vec_spec = pl.BlockSpec((1, 1, d), lambda s, j, pt: (s, 0, 0))
    small = lambda shape: pl.BlockSpec(shape, lambda s, j, pt: (0,) * len(shape))
    page = lambda g: (lambda s, j, pt: (pt[s * n_pages + j * npg + g], 0, 0, 0))
    grid_spec = pltpu.PrefetchScalarGridSpec(
        num_scalar_prefetch=1,
        grid=(n, n_pages // npg),
        in_specs=[small((1, DA_HD))] * 4 + [vec_spec, vec_spec, vec_spec]
        + [pl.BlockSpec((1, n_comp, DA_HD, PAGE_SIZE), page(g)) for g in range(npg)]
        + [pl.BlockSpec((1, PAGE_SIZE, DA_HEADS, hw), page(g)) for g in range(npg)]
        + [small((1, hw))],
        out_specs=pl.BlockSpec((1, DA_HEADS, hw), lambda s, j, pt: (s, 0, 0)),
        scratch_shapes=[
            pltpu.VMEM((n_comp, DA_HD, LANES), F32),
            pltpu.VMEM((PAGE_SIZE, PAGE_SIZE * DA_HEADS), BF16),
            pltpu.VMEM((n_comp, 1), F32),
            pltpu.VMEM((n_comp, 1), F32),
            pltpu.VMEM((n_comp, hw), F32),
        ],
    )
    return pl.pallas_call(
        functools.partial(_decode_attn_body, npg),
        grid_spec=grid_spec,
        out_shape=jax.ShapeDtypeStruct((n, DA_HEADS, hw), F32),
        compiler_params=_params(2),
        name="decode_attn",
    )(page_table.reshape(-1), *lams, row3(q), row3(k_new), row3(v_new),
      *([cache_kt] * npg), *([cache_v] * npg), subln_g)


def _attention_fused_body(nq, n_groups, npg, pt_ref, lq1_ref, lk1_ref, lq2_ref, lk2_ref,
                          qt_ref, k_ref, vt_ref, sgc_ref, q_ref, kn_ref, vn_ref, *rest):
    del pt_ref
    kt_refs, v_refs = rest[:npg], rest[npg:2 * npg]
    (sg_ref, of_ref, od_ref, q2_ref, fm_ref, fl_ref, facc_ref,
     qb_ref, ex_ref, dm_ref, dl_ref, dacc_ref) = rest[2 * npg:]
    step = pl.program_id(0)
    lam_refs = (lq1_ref, lk1_ref, lq2_ref, lk2_ref)
    _flash_step(lax.rem(step, nq), lam_refs, qt_ref, k_ref, vt_ref, sgc_ref, of_ref,
                q2_ref, fm_ref, fl_ref, facc_ref)
    _decode_step(lax.rem(step, n_groups), n_groups - 1, lam_refs, q_ref, kn_ref, vn_ref,
                 kt_refs, v_refs, sg_ref, od_ref, qb_ref, ex_ref, dm_ref, dl_ref, dacc_ref)


def _attention_fused(qt, kb, vt, q, k_new, v_new, cache_kt, cache_v, page_table, lams,
                     subln_col, subln_row):
    n_seq, d, seq = qt.shape
    nq, ta = vt.shape[1], vt.shape[3]
    n = q.shape[0]
    n_pages = page_table.shape[1]
    npg = math.gcd(DECODE_PAGES_PER_STEP, n_pages)
    n_groups = n_pages // npg
    steps = n_seq * DA_HEADS * nq
    assert steps == n * n_groups
    n_comp = 2 * DA_HEADS
    hw = 2 * DA_HD
    f_seq = lambda t: lax.div(t, DA_HEADS * nq)
    f_head = lambda t: lax.rem(lax.div(t, nq), DA_HEADS)
    f_tile = lambda t: lax.rem(t, nq)
    d_seq = lambda t: lax.div(t, n_groups)
    d_grp = lambda t: lax.rem(t, n_groups)
    row3 = lambda a: a.reshape(n, 1, d)
    small = lambda shape: pl.BlockSpec(shape, lambda t, pt: (0,) * len(shape))
    vec_spec = pl.BlockSpec((1, 1, d), lambda t, pt: (d_seq(t), 0, 0))
    page = lambda g: (lambda t, pt: (pt[d_seq(t) * n_pages + d_grp(t) * npg + g], 0, 0, 0))
    grid_spec = pltpu.PrefetchScalarGridSpec(
        num_scalar_prefetch=1,
        grid=(steps,),
        in_specs=[small((1, DA_HD))] * 4 + [
            pl.BlockSpec((1, hw, ta), lambda t, pt: (f_seq(t), f_head(t), f_tile(t))),
            pl.BlockSpec((seq, hw), lambda t, pt: (f_seq(t), f_head(t))),
            pl.BlockSpec((1, nq, hw, ta), lambda t, pt: (f_seq(t), 0, f_head(t), 0)),
            small((hw, 1)),
            vec_spec, vec_spec, vec_spec]
        + [pl.BlockSpec((1, n_comp, DA_HD, PAGE_SIZE), page(g)) for g in range(npg)]
        + [pl.BlockSpec((1, PAGE_SIZE, DA_HEADS, hw), page(g)) for g in range(npg)]
        + [small((1, hw))],
        out_specs=[
            pl.BlockSpec((ta, hw), lambda t, pt: (f_seq(t) * nq + f_tile(t), f_head(t))),
            pl.BlockSpec((1, DA_HEADS, hw), lambda t, pt: (d_seq(t), 0, 0)),
        ],
        scratch_shapes=[
            pltpu.VMEM((hw, 2 * ta), BF16),
            pltpu.VMEM((1, 2 * ta), F32),
            pltpu.VMEM((1, 2 * ta), F32),
            pltpu.VMEM((hw, 2 * ta), F32),
            pltpu.VMEM((n_comp, DA_HD, LANES), F32),
            pltpu.VMEM((PAGE_SIZE, PAGE_SIZE * DA_HEADS), BF16),
            pltpu.VMEM((n_comp, 1), F32),
            pltpu.VMEM((n_comp, 1), F32),
            pltpu.VMEM((n_comp, hw), F32),
        ],
    )
    return pl.pallas_call(
        functools.partial(_attention_fused_body, nq, n_groups, npg),
        grid_spec=grid_spec,
        out_shape=[jax.ShapeDtypeStruct((n_seq * seq, d), BF16),
                   jax.ShapeDtypeStruct((n, DA_HEADS, hw), F32)],
        compiler_params=_params(1),
        name="attention_fused",
    )(page_table.reshape(-1), *lams, qt, kb, vt, subln_col, row3(q), row3(k_new), row3(v_new),
      *([cache_kt] * npg), *([cache_v] * npg), subln_row)


def _cumsum_rows(g, tri):
    hi = g.astype(BF16)
    r1 = g - hi.astype(F32)
    mid = r1.astype(BF16)
    lo = (r1 - mid.astype(F32)).astype(BF16)
    return _mm(tri, hi) + _mm(tri, mid) + _mm(tri, lo)


def _recurrence_tile(q, k, v, g, st_ref, n_heads, dk, dv, c, post):
    t = q.shape[0]
    nc = t // c
    row = lax.broadcasted_iota(jnp.int32, (t, t), 0)
    col = lax.broadcasted_iota(jnp.int32, (t, t), 1)
    same_chunk = row // c == col // c
    tri = jnp.where(same_chunk & (row >= col), 1.0, 0.0).astype(BF16)
    all_g = _cumsum_rows(g, tri)
    q_state, q_dec, k_dec, k_tail, decay, vb = [], [], [], [], [], []
    for ci in range(nc):
        rows = slice(ci * c, (ci + 1) * c)
        big_g = all_g[rows]
        g_mid = all_g[ci * c + c // 2 - 1:ci * c + c // 2, :]
        g_last = all_g[(ci + 1) * c - 1:(ci + 1) * c, :]
        q_state.append((q[rows] * jnp.exp(big_g)).astype(BF16))
        q_dec.append((q[rows] * jnp.exp(big_g - g_mid)).astype(BF16))
        k_dec.append((k[rows] * jnp.exp(g_mid - big_g)).astype(BF16))
        k_tail.append((k[rows] * jnp.exp(g_last - big_g)).astype(BF16))
        decay.append(jnp.exp(g_last))
        vb.append(v[rows].astype(BF16))
    causal = (lax.broadcasted_iota(jnp.int32, (c, c), 0)
              >= lax.broadcasted_iota(jnp.int32, (c, c), 1))
    pairs = [(ci, h) for ci in range(nc) for h in range(n_heads)]
    ks = lambda h: slice(h * dk, (h + 1) * dk)
    vs = lambda h: slice(h * dv, (h + 1) * dv)
    scores = {p: _mm_nt(q_dec[p[0]][:, ks(p[1])], k_dec[p[0]][:, ks(p[1])]) for p in pairs}
    update = {p: _mm_tn(vb[p[0]][:, vs(p[1])], k_tail[p[0]][:, ks(p[1])]) for p in pairs}
    scores = {p: jnp.where(causal, s, 0.0).astype(BF16) for p, s in scores.items()}
    intra = {p: _mm(scores[p], vb[p[0]][:, vs(p[1])]) for p in pairs}
    entering = {}
    for h in range(n_heads):
        state = st_ref[h]
        for ci in range(nc):
            entering[ci, h] = state.astype(BF16)
            state = state * decay[ci][:, ks(h)] + update[ci, h]
        st_ref[h] = state
    inter = {p: _mm_nt(q_state[p[0]][:, ks(p[1])], entering[p]) for p in pairs}
    return jnp.concatenate(
        [jnp.concatenate([post(intra[ci, h] + inter[ci, h], h, ci) for h in range(n_heads)], axis=1)
         for ci in range(nc)], axis=0)


def _hgrn_lower_bound(lbl_ref, layer):
    logits = lbl_ref[...]
    e = jnp.exp(logits - jnp.max(logits, axis=0, keepdims=True))
    p = e / jnp.sum(e, axis=0, keepdims=True)
    return jnp.sum(p[0:layer + 1, :], axis=0, keepdims=True) - p[0:1, :]


def _hgrn_gates(p, lb, d):
    forget = lb + (1.0 - lb) * jax.nn.sigmoid(p[:, d:2 * d])
    return _silu(p[:, :d]), 1.0 - forget, p[:, 2 * d:], jnp.log(forget)


def _gla_gates(p, gk, wgk2_ref, bgk2_ref, d):
    key = d // 2
    z = _mm(gk.astype(BF16), wgk2_ref[...]) + bgk2_ref[...]
    log_sig = jnp.minimum(z, 0.0) - jnp.log(1.0 + jnp.exp(-jnp.abs(z)))
    q = p[:, :key] * ((key // GLA_HEADS) ** -0.5)
    return q, p[:, key:2 * key], p[:, 2 * key:2 * key + d], log_sig / GLA_GATE_NORM


def _recurrent_prompt_body(kind, layer, *refs):
    if kind == "hgrn":
        (x_ref, win_ref, lbl_ref, ng_ref, wout_ref, g_ref, b_ref,
         o_ref, sfin_ref, st_ref) = refs
    else:
        (x_ref, win_ref, wgk_ref, wgk2_ref, bgk2_ref, ng_ref, wout_ref, g_ref, b_ref,
         o_ref, sfin_ref, st_ref) = refs
    t = pl.program_id(1)
    tm, d = x_ref.shape
    n_heads, dv, dk = st_ref.shape

    @pl.when(t == 0)
    def _reset():
        st_ref[...] = jnp.zeros(st_ref.shape, F32)

    x = x_ref[...]
    xb = x.astype(BF16)
    p = _mm(xb, win_ref[...])
    ng = ng_ref[...]
    chunk = min(REC_CHUNK, tm)
    if kind == "hgrn":
        q, k, v, g = _hgrn_gates(p, _hgrn_lower_bound(lbl_ref, layer), d)
        post = lambda o, h, ci: _rms_norm(o, ng)
    else:
        q, k, v, g = _gla_gates(p, _mm(xb, wgk_ref[...]), wgk2_ref, bgk2_ref, d)
        gate = _silu(p[:, 2 * d:3 * d])
        post = lambda o, h, ci: (_rms_norm(o, ng)
                                 * gate[ci * chunk:(ci + 1) * chunk, h * dv:(h + 1) * dv])
    on = _recurrence_tile(q, k, v, g, st_ref, n_heads, dk, dv, chunk, post)
    y = _mm(on.astype(BF16), wout_ref[...])
    o_ref[...] = _layer_norm(ALPHA * x + y, g_ref[...], b_ref[...])

    @pl.when(t == pl.num_programs(1) - 1)
    def _state():
        for h in range(n_heads):
            sfin_ref[0, h] = st_ref[h].T


def _recurrent_prompt(kind, layer, x, n_seq, n_heads, dk, dv, weights, w_out, g, b):
    t, d = x.shape
    seq = t // n_seq
    tm = min(MIX_TILE, seq)
    nt = seq // tm
    row = lambda s, i: (s * nt + i, 0)
    w_specs = [_const_spec(w.shape) for w in weights]
    return pl.pallas_call(
        functools.partial(_recurrent_prompt_body, kind, layer),
        grid=(n_seq, nt),
        in_specs=[pl.BlockSpec((tm, d), row)] + w_specs + [
            _const_spec((d, d)), _const_spec((1, d)), _const_spec((1, d))],
        out_specs=[
            pl.BlockSpec((tm, d), row),
            pl.BlockSpec((1, n_heads, dk, dv), lambda s, i: (s, 0, 0, 0)),
        ],
        out_shape=[
            jax.ShapeDtypeStruct((t, d), F32),
            jax.ShapeDtypeStruct((n_seq, n_heads, dk, dv), F32),
        ],
        scratch_shapes=[pltpu.VMEM((n_heads, dv, dk), F32)],
        compiler_params=_params(2),
        name=kind + "_prompt",
    )(x, *weights, w_out, g, b)


def _recurrent_sample_body(kind, layer, *refs):
    if kind == "hgrn":
        (x_ref, s_ref, win_ref, lbl_ref, ng_ref, wout_ref, g_ref, b_ref,
         o_ref, snew_ref, q_scr, k_scr, v_scr, f_scr, gate_scr, on_scr) = refs
    else:
        (x_ref, s_ref, win_ref, wgk_ref, wgk2_ref, bgk2_ref, ng_ref, wout_ref, g_ref, b_ref,
         o_ref, snew_ref, q_scr, k_scr, v_scr, f_scr, gate_scr, on_scr) = refs
    n = pl.program_id(0)
    d = x_ref.shape[1]
    _, n_heads, dk, dv = s_ref.shape

    @pl.when(n == 0)
    def _project():
        xb = x_ref[...].astype(BF16)
        p = _mm(xb, win_ref[...])
        if kind == "hgrn":
            q, k, v, g = _hgrn_gates(p, _hgrn_lower_bound(lbl_ref, layer), d)
            gate_scr[...] = jnp.zeros(gate_scr.shape, F32)
        else:
            q, k, v, g = _gla_gates(p, _mm(xb, wgk_ref[...]), wgk2_ref, bgk2_ref, d)
            gate_scr[...] = p[:, 2 * d:3 * d]
        q_scr[...] = q
        k_scr[...] = k
        v_scr[...] = v
        f_scr[...] = jnp.exp(g)

    eye = _identity_bf16()
    q = q_scr[pl.ds(n, 1), :]
    k = k_scr[pl.ds(n, 1), :]
    v = v_scr[pl.ds(n, 1), :]
    f = f_scr[pl.ds(n, 1), :]
    f_hi = f.astype(BF16)
    f_r = f - f_hi.astype(F32)
    f_mid = f_r.astype(BF16)
    f_lo = (f_r - f_mid.astype(F32)).astype(BF16)
    f_col = (_lane_broadcast_columns(f_hi, eye) + _lane_broadcast_columns(f_mid, eye)
             + _lane_broadcast_columns(f_lo, eye))
    k_col = _lane_broadcast_columns(k.astype(BF16), eye)
    reps = dv // LANES
    widen = lambda a: a if reps == 1 else jnp.concatenate([a] * reps, axis=1)
    outs = []
    for h in range(n_heads):
        ks = slice(h * dk, (h + 1) * dk)
        vs = slice(h * dv, (h + 1) * dv)
        s_new = widen(f_col[ks]) * s_ref[0, h] + widen(k_col[ks]) * v[:, vs]
        snew_ref[0, h] = s_new
        q_rows = jnp.broadcast_to(q[:, ks], (2 * SUBLANES, dk)).astype(BF16)
        outs.append(_mm(q_rows, s_new.astype(BF16))[0:1, :])
    on_scr[pl.ds(n, 1), :] = jnp.concatenate(outs, axis=1)

    @pl.when(n == pl.num_programs(0) - 1)
    def _finish():
        ng = ng_ref[...]
        o = on_scr[...]
        parts = []
        for h in range(n_heads):
            oh = _rms_norm(o[:, h * dv:(h + 1) * dv], ng)
            if kind == "gla":
                oh = oh * _silu(gate_scr[:, h * dv:(h + 1) * dv])
            parts.append(oh)
        y = _mm(jnp.concatenate(parts, axis=1).astype(BF16), wout_ref[...])
        o_ref[...] = _layer_norm(ALPHA * x_ref[...] + y, g_ref[...], b_ref[...])


def _recurrent_sample(kind, layer, x, state, weights, w_out, g, b):
    n, d = x.shape
    _, n_heads, dk, dv = state.shape
    w_specs = [_const_spec(w.shape) for w in weights]
    st_spec = pl.BlockSpec((1, n_heads, dk, dv), lambda s: (s, 0, 0, 0))
    return pl.pallas_call(
        functools.partial(_recurrent_sample_body, kind, layer),
        grid=(n,),
        in_specs=[_const_spec((n, d)), st_spec] + w_specs + [
            _const_spec((d, d)), _const_spec((1, d)), _const_spec((1, d))],
        out_specs=[_const_spec((n, d)), st_spec],
        out_shape=[jax.ShapeDtypeStruct((n, d), F32), jax.ShapeDtypeStruct(state.shape, F32)],
        scratch_shapes=[
            pltpu.VMEM((n, n_heads * dk), F32),
            pltpu.VMEM((n, n_heads * dk), F32),
            pltpu.VMEM((n, n_heads * dv), F32),
            pltpu.VMEM((n, n_heads * dk), F32),
            pltpu.VMEM((n, d), F32),
            pltpu.VMEM((n, n_heads * dv), F32),
        ],
        compiler_params=_params(1),
        name=kind + "_sample",
    )(x, state, *weights, w_out, g, b)


def kernel(x_prompt, x_sample, state_conv, cache_k, cache_v, page_table, state_hgrn, state_gla,
           ffn_w_up, ffn_w_down, ln_g, ln_b,
           conv_w_in, conv_w, conv_w_out,
           da_w_in, da_lambda_q1, da_lambda_k1, da_lambda_q2, da_lambda_k2, da_subln_g, da_w_out,
           hg_w_in, hg_lb_logits, hg_norm_g, hg_w_out,
           gla_w_in, gla_w_gk2, gla_b_gk2, gla_norm_g, gla_w_out):
    n_seq, seq, d = x_prompt.shape
    n_smp = x_sample.shape[0]
    xp = x_prompt.reshape(n_seq * seq, d)
    xs = x_sample.reshape(n_smp, d)
    bf = lambda w: w.astype(BF16)
    row = lambda v: v.reshape(1, -1)
    w_up, w_down = bf(ffn_w_up), bf(ffn_w_down)

    def ffn_pair(i, half, xp, xs):
        g, b = row(ln_g[i, half * 2]), row(ln_b[i, half * 2])
        return _ffn(xp, xs, w_up, w_down, i, half, g, b)

    xp, xs = ffn_pair(0, 0, xp, xs)
    g, b = row(ln_g[0, 1]), row(ln_b[0, 1])
    cw_in, cw_out = bf(conv_w_in), bf(conv_w_out)
    xp, conv_p = _conv_prompt(xp, n_seq, cw_in, conv_w, cw_out, g, b)
    xs, u_s = _conv_sample(xs, state_conv[:, 0], state_conv[:, 1], cw_in, conv_w, cw_out, g, b)
    conv_s = jnp.stack([state_conv[:, 1], u_s], axis=1)
    xp, xs = ffn_pair(0, 1, xp, xs)

    xp, xs = ffn_pair(1, 0, xp, xs)
    g, b = row(ln_g[1, 1]), row(ln_b[1, 1])
    lams = [row(v) for v in (da_lambda_q1, da_lambda_k1, da_lambda_q2, da_lambda_k2)]
    subln = row(da_subln_g)
    dw_in, dw_out = bf(da_w_in), bf(da_w_out)
    past = page_table.shape[1] * PAGE_SIZE
    qt, kt, kb, v, vt = _da_qkv_prompt(
        xp, n_seq, dw_in[:, :d].T, dw_in[:, d:2 * d].T, dw_in[:, 2 * d:],
        _rope_tables(jnp.arange(seq, dtype=jnp.int32)))
    qs, ks, vs = _da_qkv_sample(xs, dw_in, _rope_tables(jnp.full((1,), past, jnp.int32)))
    subln_col = da_subln_g.reshape(-1, 1)
    cache_kt = cache_k.transpose(0, 2, 3, 1)
    flash_steps = n_seq * DA_HEADS * vt.shape[1]
    decode_steps = n_smp * (page_table.shape[1] // math.gcd(DECODE_PAGES_PER_STEP, page_table.shape[1]))
    if flash_steps == decode_steps:
        on, ons = _attention_fused(qt, kb, vt, qs, ks, vs, cache_kt, cache_v, page_table, lams,
                                   subln_col, subln)
    else:
        on = _flash_prompt(qt, kb, vt, lams, subln_col)
        ons = _decode_attn(qs, ks, vs, cache_kt, cache_v, page_table, lams, subln)
    k_p = kt.reshape(n_seq, 2 * DA_HEADS, DA_HD, seq).transpose(0, 3, 1, 2)
    v_p = v.reshape(n_seq, seq, DA_HEADS, 2 * DA_HD)
    xp = _proj_norm(on, xp, dw_out, g, b)
    k_s = ks.reshape(n_smp, 1, 2 * DA_HEADS, DA_HD)
    v_s = vs.reshape(n_smp, 1, DA_HEADS, 2 * DA_HD)
    xs = _proj_norm(ons.reshape(n_smp, d).astype(BF16), xs, dw_out, g, b)
    xp, xs = ffn_pair(1, 1, xp, xs)

    xp, xs = ffn_pair(2, 0, xp, xs)
    g, b = row(ln_g[2, 1]), row(ln_b[2, 1])
    hg_weights = (bf(hg_w_in), hg_lb_logits, row(hg_norm_g))
    dk = d // HG_HEADS
    xp, hg_p = _recurrent_prompt("hgrn", 2, xp, n_seq, HG_HEADS, dk, dk, hg_weights,
                                 bf(hg_w_out), g, b)
    xs, hg_s = _recurrent_sample("hgrn", 2, xs, state_hgrn, hg_weights, bf(hg_w_out), g, b)
    xp, xs = ffn_pair(2, 1, xp, xs)

    xp, xs = ffn_pair(3, 0, xp, xs)
    g, b = row(ln_g[3, 1]), row(ln_b[3, 1])
    main = 3 * d
    w_gk = jnp.pad(bf(gla_w_in[:, main:]), ((0, 0), (0, LANES - GLA_LOWRANK)))
    w_gk2 = jnp.pad(bf(gla_w_gk2), ((0, LANES - GLA_LOWRANK), (0, 0)))
    gla_weights = (bf(gla_w_in[:, :main]), w_gk, w_gk2, row(gla_b_gk2), row(gla_norm_g))
    gdk = d // 2 // GLA_HEADS
    gdv = d // GLA_HEADS
    xp, gla_p = _recurrent_prompt("gla", 3, xp, n_seq, GLA_HEADS, gdk, gdv, gla_weights,
                                  bf(gla_w_out), g, b)
    xs, gla_s = _recurrent_sample("gla", 3, xs, state_gla, gla_weights, bf(gla_w_out), g, b)
    xp, xs = ffn_pair(3, 1, xp, xs)

    return (xp.reshape(n_seq, seq, d), xs.reshape(n_smp, 1, d), conv_p, conv_s,
            k_p, v_p, k_s, v_s, hg_p, hg_s, gla_p, gla_s)
```

```python
import functools
import math

import jax
import jax.numpy as jnp
from jax import lax
from jax.experimental import pallas as pl
from jax.experimental.pallas import tpu as pltpu

F32 = jnp.float32
BF16 = jnp.bfloat16

DEPTH = 4
LN_EPS = 1e-5
RMS_EPS = 1e-6
ALPHA = (2.0 * DEPTH) ** 0.25
CONV_W = 3
DA_HEADS = 8
DA_HD = 64
ROT_DIM = DA_HD // 4
ROPE_THETA = 500000.0
DA_LAMBDA_INIT = 0.8 - 0.6 * math.exp(-0.3 * 1)
NEG_INF = -1e30
SCORE_SCALE = DA_HD ** -0.5 * math.log2(math.e)
PAGE_SIZE = 128
HG_HEADS = 8
GLA_HEADS = 4
GLA_LOWRANK = 16
GLA_GATE_NORM = 16.0

LANES = 128
SUBLANES = 8
MXU_DIM = 256
VMEM_LIMIT = 56 * 1024 * 1024

TOKEN_TILE = 512
FFN_TILE = 512
FFN_CHUNK = 1408
MIX_TILE = 256
REC_CHUNK = 64
ATT_TILE = 512
DECODE_PAGES_PER_STEP = 8


def _mm(a, b):
    return jnp.dot(a, b, preferred_element_type=F32)


def _mm_nt(a, b):
    return lax.dot_general(a, b, (((1,), (1,)), ((), ())), preferred_element_type=F32)


def _mm_tn(a, b):
    return lax.dot_general(a, b, (((0,), (0,)), ((), ())), preferred_element_type=F32)


def _layer_norm(z, g, b):
    mu = jnp.mean(z, axis=-1, keepdims=True)
    zc = z - mu
    var = jnp.mean(zc * zc, axis=-1, keepdims=True)
    return zc * lax.rsqrt(var + LN_EPS) * g + b


def _rms_norm(o, g):
    return o * lax.rsqrt(jnp.mean(o * o, axis=-1, keepdims=True) + RMS_EPS) * g


def _silu(a):
    return a * jax.nn.sigmoid(a)


def _params(n_axes):
    return pltpu.CompilerParams(dimension_semantics=("arbitrary",) * n_axes,
                                vmem_limit_bytes=VMEM_LIMIT)


def _const_spec(shape):
    nd = len(shape)
    return pl.BlockSpec(shape, lambda *_: (0,) * nd)


def _ffn_chunks(f):
    n = max(1, f // FFN_CHUNK)
    edges = [MXU_DIM * round(i * f / n / MXU_DIM) for i in range(n)] + [f]
    return list(zip(edges[:-1], edges[1:]))


def _ffn_body(nt, x_ref, xs_ref, wu_ref, wd_ref, g_ref, b_ref, o_ref, os_ref, z_ref):
    i = pl.program_id(0)
    f = wd_ref.shape[0]
    g = g_ref[...]
    b = b_ref[...]

    def residual_plus_ffn(x, norm_of=None):
        xb = x.astype(BF16)
        acc = None
        normed = None
        for c0, c1 in _ffn_chunks(f):
            a = _mm(xb, wu_ref[:, c0:c1])
            u = _mm(xb, wu_ref[:, f + c0:f + c1])
            if norm_of is not None and normed is None:
                normed = _layer_norm(norm_of, g, b)
            h = (_silu(a) * u).astype(BF16)
            part = _mm(h, wd_ref[c0:c1, :])
            acc = part if acc is None else acc + part
        return ALPHA * x + 0.5 * acc, normed

    @pl.when(i == 0)
    def _init():
        z_ref[...] = jnp.zeros(z_ref.shape, F32)

    @pl.when(i < nt)
    def _tile():
        z_new, normed = residual_plus_ffn(x_ref[...], z_ref[...])
        o_ref[...] = normed
        z_ref[...] = z_new

    @pl.when(i == nt)
    def _tail():
        o_ref[...] = _layer_norm(z_ref[...], g, b)
        zs, _ = residual_plus_ffn(xs_ref[...])
        os_ref[...] = _layer_norm(zs, g, b)


def _ffn(x, xs, w_up, w_down, layer, half, g, b):
    t, d = x.shape
    ns = xs.shape[0]
    f = w_down.shape[2]
    tm = min(FFN_TILE, t)
    nt = t // tm
    once = pl.Buffered(1)
    return pl.pallas_call(
        functools.partial(_ffn_body, nt),
        grid=(nt + 1,),
        in_specs=[
            pl.BlockSpec((tm, d), lambda i: (jnp.minimum(i, nt - 1), 0)),
            _const_spec((ns, d)),
            pl.BlockSpec((None, None, d, 2 * f), lambda i: (layer, half, 0, 0), pipeline_mode=once),
            pl.BlockSpec((None, None, f, d), lambda i: (layer, half, 0, 0), pipeline_mode=once),
            _const_spec((1, d)),
            _const_spec((1, d)),
        ],
        out_specs=[
            pl.BlockSpec((tm, d), lambda i: (jnp.maximum(i - 1, 0), 0)),
            _const_spec((ns, d)),
        ],
        out_shape=[jax.ShapeDtypeStruct((t, d), F32), jax.ShapeDtypeStruct((ns, d), F32)],
        scratch_shapes=[pltpu.VMEM((tm, d), F32)],
        compiler_params=_params(1),
        name="ffn",
    )(x, xs, w_up, w_down, g, b)


def _proj_norm_body(y_ref, x_ref, w_ref, g_ref, b_ref, o_ref):
    y = _mm(y_ref[...], w_ref[...])
    o_ref[...] = _layer_norm(ALPHA * x_ref[...] + y, g_ref[...], b_ref[...])


def _proj_norm(y, x, w, g, b):
    t, d = x.shape
    tm = min(TOKEN_TILE, t)
    return pl.pallas_call(
        _proj_norm_body,
        grid=(t // tm,),
        in_specs=[
            pl.BlockSpec((tm, d), lambda i: (i, 0)),
            pl.BlockSpec((tm, d), lambda i: (i, 0)),
            _const_spec((d, d)),
            _const_spec((1, d)),
            _const_spec((1, d)),
        ],
        out_specs=pl.BlockSpec((tm, d), lambda i: (i, 0)),
        out_shape=jax.ShapeDtypeStruct((t, d), F32),
        compiler_params=_params(1),
        name="proj_norm",
    )(y, x, w, g, b)


def _conv_prompt_body(x_ref, win_ref, cw_ref, wout_ref, g_ref, b_ref, o_ref, st_ref, ubuf_ref):
    t = pl.program_id(1)
    tm, d = x_ref.shape

    @pl.when(t == 0)
    def _reset():
        ubuf_ref[0:SUBLANES, :] = jnp.zeros((SUBLANES, d), F32)

    x = x_ref[...]
    p = _mm(x.astype(BF16), win_ref[...])
    u = p[:, d:2 * d] * p[:, 2 * d:]
    ubuf_ref[SUBLANES:, :] = u
    ue = ubuf_ref[...]
    um1 = pltpu.roll(ue, 1, 0)[SUBLANES:, :]
    um2 = pltpu.roll(ue, 2, 0)[SUBLANES:, :]
    cw = cw_ref[...]
    conv = cw[0:1, :] * um2 + cw[1:2, :] * um1 + cw[2:3, :] * u
    y = _mm((p[:, :d] * conv).astype(BF16), wout_ref[...])
    o_ref[...] = _layer_norm(ALPHA * x + y, g_ref[...], b_ref[...])
    ubuf_ref[0:SUBLANES, :] = ubuf_ref[tm:tm + SUBLANES, :]

    @pl.when(t == pl.num_programs(1) - 1)
    def _state():
        st_ref[0] = ubuf_ref[tm + SUBLANES - (CONV_W - 1):tm + SUBLANES, :]


def _conv_prompt(x, n_seq, w_in, conv_w, w_out, g, b):
    t, d = x.shape
    seq = t // n_seq
    tm = min(TOKEN_TILE, seq)
    nt = seq // tm
    return pl.pallas_call(
        _conv_prompt_body,
        grid=(n_seq, nt),
        in_specs=[
            pl.BlockSpec((tm, d), lambda s, i: (s * nt + i, 0)),
            _const_spec((d, 3 * d)),
            _const_spec((CONV_W, d)),
            _const_spec((d, d)),
            _const_spec((1, d)),
            _const_spec((1, d)),
        ],
        out_specs=[
            pl.BlockSpec((tm, d), lambda s, i: (s * nt + i, 0)),
            pl.BlockSpec((1, CONV_W - 1, d), lambda s, i: (s, 0, 0)),
        ],
        out_shape=[
            jax.ShapeDtypeStruct((t, d), F32),
            jax.ShapeDtypeStruct((n_seq, CONV_W - 1, d), F32),
        ],
        scratch_shapes=[pltpu.VMEM((tm + SUBLANES, d), F32)],
        compiler_params=_params(2),
        name="conv_prompt",
    )(x, w_in, conv_w, w_out, g, b)


def _conv_sample_body(x_ref, s0_ref, s1_ref, win_ref, cw_ref, wout_ref, g_ref, b_ref,
                      o_ref, u_ref):
    d = x_ref.shape[1]
    x = x_ref[...]
    p = _mm(x.astype(BF16), win_ref[...])
    u = p[:, d:2 * d] * p[:, 2 * d:]
    cw = cw_ref[...]
    conv = cw[0:1, :] * s0_ref[...] + cw[1:2, :] * s1_ref[...] + cw[2:3, :] * u
    y = _mm((p[:, :d] * conv).astype(BF16), wout_ref[...])
    o_ref[...] = _layer_norm(ALPHA * x + y, g_ref[...], b_ref[...])
    u_ref[...] = u


def _conv_sample(x, s0, s1, w_in, conv_w, w_out, g, b):
    n, d = x.shape
    return pl.pallas_call(
        _conv_sample_body,
        out_shape=[jax.ShapeDtypeStruct((n, d), F32), jax.ShapeDtypeStruct((n, d), F32)],
        compiler_params=pltpu.CompilerParams(vmem_limit_bytes=VMEM_LIMIT),
        name="conv_sample",
    )(x, s0, s1, w_in, conv_w, w_out, g, b)


def _rope_tables(pos):
    inv_freq = 1.0 / (ROPE_THETA ** (jnp.arange(0, ROT_DIM, 2, dtype=F32) / ROT_DIM))
    ang = pos.astype(F32)[:, None] * inv_freq[None, :]
    cos, sin = jnp.cos(ang), jnp.sin(ang)
    half = ROT_DIM // 2
    n = pos.shape[0]
    rest = DA_HD - ROT_DIM
    c64 = jnp.concatenate([cos, cos, jnp.ones((n, rest), F32)], axis=1)
    up64 = jnp.concatenate([-sin, jnp.zeros((n, DA_HD - half), F32)], axis=1)
    dn64 = jnp.concatenate([jnp.zeros((n, half), F32), sin, jnp.zeros((n, rest), F32)], axis=1)
    rep = LANES // DA_HD
    return (jnp.tile(c64, (1, rep)), jnp.tile(up64, (1, rep)), jnp.tile(dn64, (1, rep)),
            cos.T, sin.T)


def _rope_rows(x, c, s_up, s_dn):
    half = ROT_DIM // 2
    blocks = []
    for i in range(x.shape[1] // LANES):
        xb = x[:, i * LANES:(i + 1) * LANES]
        blocks.append(xb * c + pltpu.roll(xb, LANES - half, 1) * s_up + pltpu.roll(xb, half, 1) * s_dn)
    return jnp.concatenate(blocks, axis=1)


def _rope_cols(xt, ct, st):
    d, cols = xt.shape
    half = ROT_DIM // 2
    x3 = xt.reshape(d // DA_HD, DA_HD, cols)
    x1 = x3[:, 0:half, :]
    x2 = x3[:, half:ROT_DIM, :]
    x3 = jnp.concatenate([x1 * ct - x2 * st, x2 * ct + x1 * st, x3[:, ROT_DIM:, :]], axis=1)
    return x3.reshape(d, cols)


def _da_qkv_prompt_body(x_ref, wqt_ref, wkt_ref, wv_ref, ct_ref, st_ref,
                        qt_ref, kt_ref, kb_ref, v_ref, vt_ref):
    xb = x_ref[...].astype(BF16)
    ct = ct_ref[...]
    st = st_ref[...]
    qt = _rope_cols(_mm_nt(wqt_ref[...], xb), ct, st)
    qt_ref[0] = (qt * SCORE_SCALE).astype(BF16)
    kt = _rope_cols(_mm_nt(wkt_ref[...], xb), ct, st)
    kt_ref[0] = kt
    kb_ref[...] = kt.T.astype(BF16)
    v = _mm(xb, wv_ref[...])
    v_ref[...] = v
    vt_ref[0, 0] = v.T.astype(BF16)


def _da_qkv_prompt(x, n_seq, wqt, wkt, wv, tables):
    t, d = x.shape
    seq = t // n_seq
    tm = min(ATT_TILE, seq)
    nt = seq // tm
    ct, st = tables[3:]
    half = ROT_DIM // 2
    row = lambda s, i: (s * nt + i, 0)
    return pl.pallas_call(
        _da_qkv_prompt_body,
        grid=(n_seq, nt),
        in_specs=[
            pl.BlockSpec((tm, d), row),
            _const_spec((d, d)),
            _const_spec((d, d)),
            _const_spec((d, d)),
            pl.BlockSpec((half, tm), lambda s, i: (0, i)),
            pl.BlockSpec((half, tm), lambda s, i: (0, i)),
        ],
        out_specs=[
            pl.BlockSpec((1, d, tm), lambda s, i: (s, 0, i)),
            pl.BlockSpec((1, d, tm), lambda s, i: (s, 0, i)),
            pl.BlockSpec((tm, d), row),
            pl.BlockSpec((tm, d), row),
            pl.BlockSpec((1, 1, d, tm), lambda s, i: (s, i, 0, 0)),
        ],
        out_shape=[
            jax.ShapeDtypeStruct((n_seq, d, seq), BF16),
            jax.ShapeDtypeStruct((n_seq, d, seq), F32),
            jax.ShapeDtypeStruct((t, d), BF16),
            jax.ShapeDtypeStruct((t, d), F32),
            jax.ShapeDtypeStruct((n_seq, nt, d, tm), BF16),
        ],
        compiler_params=_params(2),
        name="da_qkv_prompt",
    )(x, wqt, wkt, wv, ct, st)


def _diff_lambda(lq1_ref, lk1_ref, lq2_ref, lk2_ref):
    s1 = jnp.sum(lq1_ref[...] * lk1_ref[...], axis=-1, keepdims=True)
    s2 = jnp.sum(lq2_ref[...] * lk2_ref[...], axis=-1, keepdims=True)
    return jnp.exp(s1) - jnp.exp(s2) + DA_LAMBDA_INIT


def _flash_body(lq1_ref, lk1_ref, lq2_ref, lk2_ref, *refs):
    _flash_step(pl.program_id(2), (lq1_ref, lk1_ref, lq2_ref, lk2_ref), *refs)


def _flash_step(qi, lam_refs, qt_ref, k_ref, vt_ref, sg_ref, o_ref, q2_ref, m_ref, l_ref, acc_ref):
    tq = qt_ref.shape[2]
    tk = vt_ref.shape[3]
    n_col = 2 * tq
    cb = min(2 * MXU_DIM, tq)

    qt = qt_ref[0]
    sub = lax.broadcasted_iota(jnp.int32, qt.shape, 0)
    zero = jnp.zeros_like(qt)
    q2_ref[:, 0:tq] = jnp.where(sub < DA_HD, qt, zero)
    q2_ref[:, tq:] = jnp.where(sub >= DA_HD, qt, zero)
    m_ref[...] = jnp.full(m_ref.shape, NEG_INF, F32)
    l_ref[...] = jnp.zeros(l_ref.shape, F32)
    acc_ref[...] = jnp.zeros(acc_ref.shape, F32)

    starts = list(range(0, n_col, cb))

    def kv_tiles(kjs, diagonal):
        k_tiles = [k_ref[pl.ds(pl.multiple_of(kj * tk, tk), tk), :] for kj in kjs]
        vt_tiles = [vt_ref[0, kj] for kj in kjs]
        items = [(t, c0) for t in range(len(kjs)) for c0 in starts]
        n_items = len(items)

        def score(n):
            t, c0 = items[n]
            return _mm(k_tiles[t], q2_ref[:, c0:c0 + cb])

        def max_pass(n, st):
            c0 = items[n][1]
            if diagonal:
                key = lax.broadcasted_iota(jnp.int32, st.shape, 0)
                qry = lax.broadcasted_iota(jnp.int32, st.shape, 1) + (c0 % tq)
                st = jnp.where(key <= qry, st, NEG_INF)
            m_prev = m_ref[:, c0:c0 + cb]
            return st, m_prev, jnp.maximum(m_prev, jnp.max(st, axis=0, keepdims=True))

        def exp_pass(n, st, m_prev, m_new):
            t, c0 = items[n]
            cs = slice(c0, c0 + cb)
            alpha = jnp.exp2(m_prev - m_new)
            pt = jnp.exp2(st - m_new)
            l_ref[:, cs] = alpha * l_ref[:, cs] + jnp.sum(pt, axis=0, keepdims=True)
            acc_ref[:, cs] = alpha * acc_ref[:, cs] + _mm(vt_tiles[t], pt.astype(BF16))
            m_ref[:, cs] = m_new

        raw = {n: score(n) for n in range(min(2, n_items))}
        ready = {0: max_pass(0, raw.pop(0))}
        for n in range(n_items):
            if n + 2 < n_items:
                raw[n + 2] = score(n + 2)
            if n + 1 < n_items:
                ready[n + 1] = max_pass(n + 1, raw.pop(n + 1))
            exp_pass(n, *ready.pop(n))

    def tile_pair(jj, carry):
        kv_tiles([2 * jj, 2 * jj + 1], False)
        return carry

    lax.fori_loop(0, lax.shift_right_logical(qi, 1), tile_pair, 0)

    @pl.when(jnp.bitwise_and(qi, 1) == 1)
    def _odd_tile():
        kv_tiles([qi - 1], False)

    kv_tiles([qi], True)

    lam = _diff_lambda(*lam_refs)
    o = acc_ref[...] / l_ref[...]
    o = o[:, 0:tq] - lam * o[:, tq:]
    r = lax.rsqrt(jnp.mean(o * o, axis=0, keepdims=True) + RMS_EPS)
    o = o * r * sg_ref[...] * (1.0 - DA_LAMBDA_INIT)
    o_ref[...] = o.T.astype(BF16)


def _flash_prompt(qt, kb, vt, lams, subln_col):
    n_seq, d, seq = qt.shape
    nq, ta = vt.shape[1], vt.shape[3]
    hw = 2 * DA_HD
    return pl.pallas_call(
        _flash_body,
        grid=(n_seq, DA_HEADS, nq),
        in_specs=[_const_spec((1, DA_HD))] * 4 + [
            pl.BlockSpec((1, hw, ta), lambda s, h, i: (s, h, i)),
            pl.BlockSpec((seq, hw), lambda s, h, i: (s, h)),
            pl.BlockSpec((1, nq, hw, ta), lambda s, h, i: (s, 0, h, 0)),
            _const_spec((hw, 1)),
        ],
        out_specs=pl.BlockSpec((ta, hw), lambda s, h, i: (s * nq + i, h)),
        out_shape=jax.ShapeDtypeStruct((n_seq * seq, d), BF16),
        scratch_shapes=[
            pltpu.VMEM((hw, 2 * ta), BF16),
            pltpu.VMEM((1, 2 * ta), F32),
            pltpu.VMEM((1, 2 * ta), F32),
            pltpu.VMEM((hw, 2 * ta), F32),
        ],
        compiler_params=_params(3),
        name="flash_prompt",
    )(*lams, qt, kb, vt, subln_col)


def _da_qkv_sample_body(x_ref, w_ref, c_ref, up_ref, dn_ref, q_ref, k_ref, v_ref):
    d = x_ref.shape[1]
    p = _mm(x_ref[...].astype(BF16), w_ref[...])
    c, up, dn = c_ref[...], up_ref[...], dn_ref[...]
    q_ref[...] = _rope_rows(p[:, :d], c, up, dn) * SCORE_SCALE
    k_ref[...] = _rope_rows(p[:, d:2 * d], c, up, dn)
    v_ref[...] = p[:, 2 * d:]


def _da_qkv_sample(x, w_in, tables):
    n, d = x.shape
    c, up, dn = tables[:3]
    return pl.pallas_call(
        _da_qkv_sample_body,
        out_shape=[jax.ShapeDtypeStruct((n, d), F32)] * 3,
        compiler_params=pltpu.CompilerParams(vmem_limit_bytes=VMEM_LIMIT),
        name="da_qkv_sample",
    )(x, w_in, c, up, dn)


def _lane_broadcast_columns(vec_bf, eye):
    w = vec_bf.shape[1]
    rows = jnp.broadcast_to(vec_bf, (LANES, w))
    return jnp.concatenate(
        [_mm_nt(eye, rows[:, i * LANES:(i + 1) * LANES]) for i in range(w // LANES)], axis=0)


def _identity_bf16():
    r = lax.broadcasted_iota(jnp.int32, (LANES, LANES), 0)
    c = lax.broadcasted_iota(jnp.int32, (LANES, LANES), 1)
    return jnp.where(r == c, 1.0, 0.0).astype(BF16)


def _decode_attn_body(npg, pt_ref, lq1_ref, lk1_ref, lq2_ref, lk2_ref, q_ref, kn_ref, vn_ref, *rest):
    del pt_ref
    _decode_step(pl.program_id(1), pl.num_programs(1) - 1, (lq1_ref, lk1_ref, lq2_ref, lk2_ref),
                 q_ref, kn_ref, vn_ref, rest[:npg], rest[npg:2 * npg], *rest[2 * npg:])


def _decode_step(j, j_last, lam_refs, q_ref, kn_ref, vn_ref, kt_refs, v_refs,
                 sg_ref, o_ref, qb_ref, ex_ref, m_ref, l_ref, acc_ref):
    npg = len(kt_refs)
    n_comp = 2 * DA_HEADS
    flat = PAGE_SIZE * DA_HEADS

    @pl.when(j == 0)
    def _init():
        qb = _lane_broadcast_columns(q_ref[0].astype(BF16), _identity_bf16())
        qb_ref[...] = qb.reshape(qb_ref.shape)
        key = lax.broadcasted_iota(jnp.int32, (PAGE_SIZE, flat), 0)
        col = lax.broadcasted_iota(jnp.int32, (PAGE_SIZE, flat), 1)
        ex_ref[...] = jnp.where(col // DA_HEADS == key, 1.0, 0.0).astype(BF16)
        m_ref[...] = jnp.full(m_ref.shape, NEG_INF, F32)
        l_ref[...] = jnp.zeros(l_ref.shape, F32)
        acc_ref[...] = jnp.zeros(acc_ref.shape, F32)

    def scores(kt_of_comp):
        parts = []
        for comp in range(2):
            qb = qb_ref[pl.ds(comp, DA_HEADS, stride=2)]
            parts.append(jnp.sum(qb * kt_of_comp(comp), axis=1))
        return jnp.concatenate(parts, axis=0)

    def online_update(s, pv_of_p):
        m_prev = m_ref[...]
        m_new = jnp.maximum(m_prev, jnp.max(s, axis=-1, keepdims=True))
        alpha = jnp.exp2(m_prev - m_new)
        p = jnp.exp2(s - m_new)
        l_ref[...] = alpha * l_ref[...] + jnp.sum(p, axis=-1, keepdims=True)
        acc_ref[...] = alpha * acc_ref[...] + pv_of_p(p)
        m_ref[...] = m_new

    def pages_pv(p):
        row = lax.broadcasted_iota(jnp.int32, (n_comp, flat), 0)
        col = lax.broadcasted_iota(jnp.int32, (n_comp, flat), 1)
        own_head = col % DA_HEADS == row % DA_HEADS
        pb = p.astype(BF16)
        ex = ex_ref[...]
        spread = [_mm(pb[:, g * PAGE_SIZE:(g + 1) * PAGE_SIZE], ex) for g in range(npg)]
        spread = [jnp.where(own_head, s, 0.0).astype(BF16) for s in spread]
        parts = [_mm(spread[g], v_refs[g][0].reshape(flat, LANES).astype(BF16))
                 for g in range(npg)]
        return functools.reduce(lambda a, b: a + b, parts)

    def page_scores():
        rows = []
        for r in range(n_comp):
            c = 2 * (r % DA_HEADS) + r // DA_HEADS
            qc = qb_ref[c]
            rows.append(jnp.concatenate(
                [jnp.sum(qc * kt[0, c], axis=0, keepdims=True) for kt in kt_refs], axis=1))
        return jnp.concatenate(rows, axis=0)

    online_update(page_scores(), pages_pv)

    @pl.when(j == j_last)
    def _finish():
        kb = _lane_broadcast_columns(kn_ref[0].astype(BF16), _identity_bf16())
        kb = kb.reshape(DA_HEADS, 2, DA_HD, LANES)
        s_self = scores(lambda comp: kb[:, comp])
        lane = lax.broadcasted_iota(jnp.int32, (n_comp, LANES), 1)
        s_self = jnp.where(lane == 0, s_self, NEG_INF)
        vn = vn_ref[0]
        v_rows = jnp.concatenate(
            [vn[:, h * LANES:(h + 1) * LANES] for h in range(DA_HEADS)] * 2, axis=0)
        online_update(s_self, lambda p: jnp.sum(p, axis=-1, keepdims=True) * v_rows)
        lam = _diff_lambda(*lam_refs)
        o = acc_ref[...] / l_ref[...]
        o = o[0:DA_HEADS, :] - lam * o[DA_HEADS:, :]
        o_ref[0] = _rms_norm(o, sg_ref[...]) * (1.0 - DA_LAMBDA_INIT)


def _decode_attn(q, k_new, v_new, cache_kt, cache_v, page_table, lams, subln_g):
    n, d = q.shape
    n_pages = page_table.shape[1]
    npg = math.gcd(DECODE_PAGES_PER_STEP, n_pages)
    n_comp = 2 * DA_HEADS
    hw = 2 * DA_HD
    row3 = lambda a: a.reshape(n, 1, d)
    vec_spec = pl.BlockSpec((1, 1, d), lambda s, j, pt: (s, 0, 0))
    small = lambda shape: pl.BlockSpec(shape, lambda s, j, pt: (0,) * len(shape))
    page = lambda g: (lambda s, j, pt: (pt[s * n_pages + j * npg + g], 0, 0, 0))
    grid_spec = pltpu.PrefetchScalarGridSpec(
        num_scalar_prefetch=1,
        grid=(n, n_pages // npg),
        in_specs=[small((1, DA_HD))] * 4 + [vec_spec, vec_spec, vec_spec]
        + [pl.BlockSpec((1, n_comp, DA_HD, PAGE_SIZE), page(g)) for g in range(npg)]
        + [pl.BlockSpec((1, PAGE_SIZE, DA_HEADS, hw), page(g)) for g in range(npg)]
        + [small((1, hw))],
        out_specs=pl.BlockSpec((1, DA_HEADS, hw), lambda s, j, pt: (s, 0, 0)),
        scratch_shapes=[
            pltpu.VMEM((n_comp, DA_HD, LANES), F32),
            pltpu.VMEM((PAGE_SIZE, PAGE_SIZE * DA_HEADS), BF16),
            pltpu.VMEM((n_comp, 1), F32),
            pltpu.VMEM((n_comp, 1), F32),
            pltpu.VMEM((n_comp, hw), F32),
        ],
    )
    return pl.pallas_call(
        functools.partial(_decode_attn_body, npg),
        grid_spec=grid_spec,
        out_shape=jax.ShapeDtypeStruct((n, DA_HEADS, hw), F32),
        compiler_params=_params(2),
        name="decode_attn",
    )(page_table.reshape(-1), *lams, row3(q), row3(k_new), row3(v_new),
      *([cache_kt] * npg), *([cache_v] * npg), subln_g)


def _attention_fused_body(nq, n_groups, npg, pt_ref, lq1_ref, lk1_ref, lq2_ref, lk2_ref,
                          qt_ref, k_ref, vt_ref, sgc_ref, q_ref, kn_ref, vn_ref, *rest):
    del pt_ref
    kt_refs, v_refs = rest[:npg], rest[npg:2 * npg]
    (sg_ref, of_ref, od_ref, q2_ref, fm_ref, fl_ref, facc_ref,
     qb_ref, ex_ref, dm_ref, dl_ref, dacc_ref) = rest[2 * npg:]
    step = pl.program_id(0)
    lam_refs = (lq1_ref, lk1_ref, lq2_ref, lk2_ref)
    _flash_step(lax.rem(step, nq), lam_refs, qt_ref, k_ref, vt_ref, sgc_ref, of_ref,
                q2_ref, fm_ref, fl_ref, facc_ref)
    _decode_step(lax.rem(step, n_groups), n_groups - 1, lam_refs, q_ref, kn_ref, vn_ref,
                 kt_refs, v_refs, sg_ref, od_ref, qb_ref, ex_ref, dm_ref, dl_ref, dacc_ref)


def _attention_fused(qt, kb, vt, q, k_new, v_new, cache_kt, cache_v, page_table, lams,
                     subln_col, subln_row):
    n_seq, d, seq = qt.shape
    nq, ta = vt.shape[1], vt.shape[3]
    n = q.shape[0]
    n_pages = page_table.shape[1]
    npg = math.gcd(DECODE_PAGES_PER_STEP, n_pages)
    n_groups = n_pages // npg
    steps = n_seq * DA_HEADS * nq
    assert steps == n * n_groups
    n_comp = 2 * DA_HEADS
    hw = 2 * DA_HD
    f_seq = lambda t: lax.div(t, DA_HEADS * nq)
    f_head = lambda t: lax.rem(lax.div(t, nq), DA_HEADS)
    f_tile = lambda t: lax.rem(t, nq)
    d_seq = lambda t: lax.div(t, n_groups)
    d_grp = lambda t: lax.rem(t, n_groups)
    row3 = lambda a: a.reshape(n, 1, d)
    small = lambda shape: pl.BlockSpec(shape, lambda t, pt: (0,) * len(shape))
    vec_spec = pl.BlockSpec((1, 1, d), lambda t, pt: (d_seq(t), 0, 0))
    page = lambda g: (lambda t, pt: (pt[d_seq(t) * n_pages + d_grp(t) * npg + g], 0, 0, 0))
    grid_spec = pltpu.PrefetchScalarGridSpec(
        num_scalar_prefetch=1,
        grid=(steps,),
        in_specs=[small((1, DA_HD))] * 4 + [
            pl.BlockSpec((1, hw, ta), lambda t, pt: (f_seq(t), f_head(t), f_tile(t))),
            pl.BlockSpec((seq, hw), lambda t, pt: (f_seq(t), f_head(t))),
            pl.BlockSpec((1, nq, hw, ta), lambda t, pt: (f_seq(t), 0, f_head(t), 0)),
            small((hw, 1)),
            vec_spec, vec_spec, vec_spec]
        + [pl.BlockSpec((1, n_comp, DA_HD, PAGE_SIZE), page(g)) for g in range(npg)]
        + [pl.BlockSpec((1, PAGE_SIZE, DA_HEADS, hw), page(g)) for g in range(npg)]
        + [small((1, hw))],
        out_specs=[
            pl.BlockSpec((ta, hw), lambda t, pt: (f_seq(t) * nq + f_tile(t), f_head(t))),
            pl.BlockSpec((1, DA_HEADS, hw), lambda t, pt: (d_seq(t), 0, 0)),
        ],
        scratch_shapes=[
            pltpu.VMEM((hw, 2 * ta), BF16),
            pltpu.VMEM((1, 2 * ta), F32),
            pltpu.VMEM((1, 2 * ta), F32),
            pltpu.VMEM((hw, 2 * ta), F32),
            pltpu.VMEM((n_comp, DA_HD, LANES), F32),
            pltpu.VMEM((PAGE_SIZE, PAGE_SIZE * DA_HEADS), BF16),
            pltpu.VMEM((n_comp, 1), F32),
            pltpu.VMEM((n_comp, 1), F32),
            pltpu.VMEM((n_comp, hw), F32),
        ],
    )
    return pl.pallas_call(
        functools.partial(_attention_fused_body, nq, n_groups, npg),
        grid_spec=grid_spec,
        out_shape=[jax.ShapeDtypeStruct((n_seq * seq, d), BF16),
                   jax.ShapeDtypeStruct((n, DA_HEADS, hw), F32)],
        compiler_params=_params(1),
        name="attention_fused",
    )(page_table.reshape(-1), *lams, qt, kb, vt, subln_col, row3(q), row3(k_new), row3(v_new),
      *([cache_kt] * npg), *([cache_v] * npg), subln_row)


def _cumsum_rows(g, tri):
    hi = g.astype(BF16)
    r1 = g - hi.astype(F32)
    mid = r1.astype(BF16)
    lo = (r1 - mid.astype(F32)).astype(BF16)
    return _mm(tri, hi) + _mm(tri, mid) + _mm(tri, lo)


def _recurrence_tile(q, k, v, g, st_ref, n_heads, dk, dv, c, post):
    t = q.shape[0]
    nc = t // c
    row = lax.broadcasted_iota(jnp.int32, (t, t), 0)
    col = lax.broadcasted_iota(jnp.int32, (t, t), 1)
    same_chunk = row // c == col // c
    tri = jnp.where(same_chunk & (row >= col), 1.0, 0.0).astype(BF16)
    all_g = _cumsum_rows(g, tri)
    q_state, q_dec, k_dec, k_tail, decay, vb = [], [], [], [], [], []
    for ci in range(nc):
        rows = slice(ci * c, (ci + 1) * c)
        big_g = all_g[rows]
        g_mid = all_g[ci * c + c // 2 - 1:ci * c + c // 2, :]
        g_last = all_g[(ci + 1) * c - 1:(ci + 1) * c, :]
        q_state.append((q[rows] * jnp.exp(big_g)).astype(BF16))
        q_dec.append((q[rows] * jnp.exp(big_g - g_mid)).astype(BF16))
        k_dec.append((k[rows] * jnp.exp(g_mid - big_g)).astype(BF16))
        k_tail.append((k[rows] * jnp.exp(g_last - big_g)).astype(BF16))
        decay.append(jnp.exp(g_last))
        vb.append(v[rows].astype(BF16))
    causal = (lax.broadcasted_iota(jnp.int32, (c, c), 0)
              >= lax.broadcasted_iota(jnp.int32, (c, c), 1))
    pairs = [(ci, h) for ci in range(nc) for h in range(n_heads)]
    ks = lambda h: slice(h * dk, (h + 1) * dk)
    vs = lambda h: slice(h * dv, (h + 1) * dv)
    scores = {p: _mm_nt(q_dec[p[0]][:, ks(p[1])], k_dec[p[0]][:, ks(p[1])]) for p in pairs}
    update = {p: _mm_tn(vb[p[0]][:, vs(p[1])], k_tail[p[0]][:, ks(p[1])]) for p in pairs}
    scores = {p: jnp.where(causal, s, 0.0).astype(BF16) for p, s in scores.items()}
    intra = {p: _mm(scores[p], vb[p[0]][:, vs(p[1])]) for p in pairs}
    entering = {}
    for h in range(n_heads):
        state = st_ref[h]
        for ci in range(nc):
            entering[ci, h] = state.astype(BF16)
            state = state * decay[ci][:, ks(h)] + update[ci, h]
        st_ref[h] = state
    inter = {p: _mm_nt(q_state[p[0]][:, ks(p[1])], entering[p]) for p in pairs}
    return jnp.concatenate(
        [jnp.concatenate([post(intra[ci, h] + inter[ci, h], h, ci) for h in range(n_heads)], axis=1)
         for ci in range(nc)], axis=0)


def _hgrn_lower_bound(lbl_ref, layer):
    logits = lbl_ref[...]
    e = jnp.exp(logits - jnp.max(logits, axis=0, keepdims=True))
    p = e / jnp.sum(e, axis=0, keepdims=True)
    return jnp.sum(p[0:layer + 1, :], axis=0, keepdims=True) - p[0:1, :]


def _hgrn_gates(p, lb, d):
    forget = lb + (1.0 - lb) * jax.nn.sigmoid(p[:, d:2 * d])
    return _silu(p[:, :d]), 1.0 - forget, p[:, 2 * d:], jnp.log(forget)


def _gla_gates(p, gk, wgk2_ref, bgk2_ref, d):
    key = d // 2
    z = _mm(gk.astype(BF16), wgk2_ref[...]) + bgk2_ref[...]
    log_sig = jnp.minimum(z, 0.0) - jnp.log(1.0 + jnp.exp(-jnp.abs(z)))
    q = p[:, :key] * ((key // GLA_HEADS) ** -0.5)
    return q, p[:, key:2 * key], p[:, 2 * key:2 * key + d], log_sig / GLA_GATE_NORM


def _recurrent_prompt_body(kind, layer, *refs):
    if kind == "hgrn":
        (x_ref, win_ref, lbl_ref, ng_ref, wout_ref, g_ref, b_ref,
         o_ref, sfin_ref, st_ref) = refs
    else:
        (x_ref, win_ref, wgk_ref, wgk2_ref, bgk2_ref, ng_ref, wout_ref, g_ref, b_ref,
         o_ref, sfin_ref, st_ref) = refs
    t = pl.program_id(1)
    tm, d = x_ref.shape
    n_heads, dv, dk = st_ref.shape

    @pl.when(t == 0)
    def _reset():
        st_ref[...] = jnp.zeros(st_ref.shape, F32)

    x = x_ref[...]
    xb = x.astype(BF16)
    p = _mm(xb, win_ref[...])
    ng = ng_ref[...]
    chunk = min(REC_CHUNK, tm)
    if kind == "hgrn":
        q, k, v, g = _hgrn_gates(p, _hgrn_lower_bound(lbl_ref, layer), d)
        post = lambda o, h, ci: _rms_norm(o, ng)
    else:
        q, k, v, g = _gla_gates(p, _mm(xb, wgk_ref[...]), wgk2_ref, bgk2_ref, d)
        gate = _silu(p[:, 2 * d:3 * d])
        post = lambda o, h, ci: (_rms_norm(o, ng)
                                 * gate[ci * chunk:(ci + 1) * chunk, h * dv:(h + 1) * dv])
    on = _recurrence_tile(q, k, v, g, st_ref, n_heads, dk, dv, chunk, post)
    y = _mm(on.astype(BF16), wout_ref[...])
    o_ref[...] = _layer_norm(ALPHA * x + y, g_ref[...], b_ref[...])

    @pl.when(t == pl.num_programs(1) - 1)
    def _state():
        for h in range(n_heads):
            sfin_ref[0, h] = st_ref[h].T


def _recurrent_prompt(kind, layer, x, n_seq, n_heads, dk, dv, weights, w_out, g, b):
    t, d = x.shape
    seq = t // n_seq
    tm = min(MIX_TILE, seq)
    nt = seq // tm
    row = lambda s, i: (s * nt + i, 0)
    w_specs = [_const_spec(w.shape) for w in weights]
    return pl.pallas_call(
        functools.partial(_recurrent_prompt_body, kind, layer),
        grid=(n_seq, nt),
        in_specs=[pl.BlockSpec((tm, d), row)] + w_specs + [
            _const_spec((d, d)), _const_spec((1, d)), _const_spec((1, d))],
        out_specs=[
            pl.BlockSpec((tm, d), row),
            pl.BlockSpec((1, n_heads, dk, dv), lambda s, i: (s, 0, 0, 0)),
        ],
        out_shape=[
            jax.ShapeDtypeStruct((t, d), F32),
            jax.ShapeDtypeStruct((n_seq, n_heads, dk, dv), F32),
        ],
        scratch_shapes=[pltpu.VMEM((n_heads, dv, dk), F32)],
        compiler_params=_params(2),
        name=kind + "_prompt",
    )(x, *weights, w_out, g, b)


def _recurrent_sample_body(kind, layer, *refs):
    if kind == "hgrn":
        (x_ref, s_ref, win_ref, lbl_ref, ng_ref, wout_ref, g_ref, b_ref,
         o_ref, snew_ref, q_scr, k_scr, v_scr, f_scr, gate_scr, on_scr) = refs
    else:
        (x_ref, s_ref, win_ref, wgk_ref, wgk2_ref, bgk2_ref, ng_ref, wout_ref, g_ref, b_ref,
         o_ref, snew_ref, q_scr, k_scr, v_scr, f_scr, gate_scr, on_scr) = refs
    n = pl.program_id(0)
    d = x_ref.shape[1]
    _, n_heads, dk, dv = s_ref.shape

    @pl.when(n == 0)
    def _project():
        xb = x_ref[...].astype(BF16)
        p = _mm(xb, win_ref[...])
        if kind == "hgrn":
            q, k, v, g = _hgrn_gates(p, _hgrn_lower_bound(lbl_ref, layer), d)
            gate_scr[...] = jnp.zeros(gate_scr.shape, F32)
        else:
            q, k, v, g = _gla_gates(p, _mm(xb, wgk_ref[...]), wgk2_ref, bgk2_ref, d)
            gate_scr[...] = p[:, 2 * d:3 * d]
        q_scr[...] = q
        k_scr[...] = k
        v_scr[...] = v
        f_scr[...] = jnp.exp(g)

    eye = _identity_bf16()
    q = q_scr[pl.ds(n, 1), :]
    k = k_scr[pl.ds(n, 1), :]
    v = v_scr[pl.ds(n, 1), :]
    f = f_scr[pl.ds(n, 1), :]
    f_hi = f.astype(BF16)
    f_r = f - f_hi.astype(F32)
    f_mid = f_r.astype(BF16)
    f_lo = (f_r - f_mid.astype(F32)).astype(BF16)
    f_col = (_lane_broadcast_columns(f_hi, eye) + _lane_broadcast_columns(f_mid, eye)
             + _lane_broadcast_columns(f_lo, eye))
    k_col = _lane_broadcast_columns(k.astype(BF16), eye)
    reps = dv // LANES
    widen = lambda a: a if reps == 1 else jnp.concatenate([a] * reps, axis=1)
    outs = []
    for h in range(n_heads):
        ks = slice(h * dk, (h + 1) * dk)
        vs = slice(h * dv, (h + 1) * dv)
        s_new = widen(f_col[ks]) * s_ref[0, h] + widen(k_col[ks]) * v[:, vs]
        snew_ref[0, h] = s_new
        q_rows = jnp.broadcast_to(q[:, ks], (2 * SUBLANES, dk)).astype(BF16)
        outs.append(_mm(q_rows, s_new.astype(BF16))[0:1, :])
    on_scr[pl.ds(n, 1), :] = jnp.concatenate(outs, axis=1)

    @pl.when(n == pl.num_programs(0) - 1)
    def _finish():
        ng = ng_ref[...]
        o = on_scr[...]
        parts = []
        for h in range(n_heads):
            oh = _rms_norm(o[:, h * dv:(h + 1) * dv], ng)
            if kind == "gla":
                oh = oh * _silu(gate_scr[:, h * dv:(h + 1) * dv])
            parts.append(oh)
        y = _mm(jnp.concatenate(parts, axis=1).astype(BF16), wout_ref[...])
        o_ref[...] = _layer_norm(ALPHA * x_ref[...] + y, g_ref[...], b_ref[...])


def _recurrent_sample(kind, layer, x, state, weights, w_out, g, b):
    n, d = x.shape
    _, n_heads, dk, dv = state.shape
    w_specs = [_const_spec(w.shape) for w in weights]
    st_spec = pl.BlockSpec((1, n_heads, dk, dv), lambda s: (s, 0, 0, 0))
    return pl.pallas_call(
        functools.partial(_recurrent_sample_body, kind, layer),
        grid=(n,),
        in_specs=[_const_spec((n, d)), st_spec] + w_specs + [
            _const_spec((d, d)), _const_spec((1, d)), _const_spec((1, d))],
        out_specs=[_const_spec((n, d)), st_spec],
        out_shape=[jax.ShapeDtypeStruct((n, d), F32), jax.ShapeDtypeStruct(state.shape, F32)],
        scratch_shapes=[
            pltpu.VMEM((n, n_heads * dk), F32),
            pltpu.VMEM((n, n_heads * dk), F32),
            pltpu.VMEM((n, n_heads * dv), F32),
            pltpu.VMEM((n, n_heads * dk), F32),
            pltpu.VMEM((n, d), F32),
            pltpu.VMEM((n, n_heads * dv), F32),
        ],
        compiler_params=_params(1),
        name=kind + "_sample",
    )(x, state, *weights, w_out, g, b)


def kernel(x_prompt, x_sample, state_conv, cache_k, cache_v, page_table, state_hgrn, state_gla,
           ffn_w_up, ffn_w_down, ln_g, ln_b,
           conv_w_in, conv_w, conv_w_out,
           da_w_in, da_lambda_q1, da_lambda_k1, da_lambda_q2, da_lambda_k2, da_subln_g, da_w_out,
           hg_w_in, hg_lb_logits, hg_norm_g, hg_w_out,
           gla_w_in, gla_w_gk2, gla_b_gk2, gla_norm_g, gla_w_out):
    n_seq, seq, d = x_prompt.shape
    n_smp = x_sample.shape[0]
    xp = x_prompt.reshape(n_seq * seq, d)
    xs = x_sample.reshape(n_smp, d)
    bf = lambda w: w.astype(BF16)
    row = lambda v: v.reshape(1, -1)
    w_up, w_down = bf(ffn_w_up), bf(ffn_w_down)

    def ffn_pair(i, half, xp, xs):
        g, b = row(ln_g[i, half * 2]), row(ln_b[i, half * 2])
        return _ffn(xp, xs, w_up, w_down, i, half, g, b)

    xp, xs = ffn_pair(0, 0, xp, xs)
    g, b = row(ln_g[0, 1]), row(ln_b[0, 1])
    cw_in, cw_out = bf(conv_w_in), bf(conv_w_out)
    xp, conv_p = _conv_prompt(xp, n_seq, cw_in, conv_w, cw_out, g, b)
    xs, u_s = _conv_sample(xs, state_conv[:, 0], state_conv[:, 1], cw_in, conv_w, cw_out, g, b)
    conv_s = jnp.stack([state_conv[:, 1], u_s], axis=1)
    xp, xs = ffn_pair(0, 1, xp, xs)

    xp, xs = ffn_pair(1, 0, xp, xs)
    g, b = row(ln_g[1, 1]), row(ln_b[1, 1])
    lams = [row(v) for v in (da_lambda_q1, da_lambda_k1, da_lambda_q2, da_lambda_k2)]
    subln = row(da_subln_g)
    dw_in, dw_out = bf(da_w_in), bf(da_w_out)
    past = page_table.shape[1] * PAGE_SIZE
    qt, kt, kb, v, vt = _da_qkv_prompt(
        xp, n_seq, dw_in[:, :d].T, dw_in[:, d:2 * d].T, dw_in[:, 2 * d:],
        _rope_tables(jnp.arange(seq, dtype=jnp.int32)))
    qs, ks, vs = _da_qkv_sample(xs, dw_in, _rope_tables(jnp.full((1,), past, jnp.int32)))
    subln_col = da_subln_g.reshape(-1, 1)
    cache_kt = cache_k.transpose(0, 2, 3, 1)
    flash_steps = n_seq * DA_HEADS * vt.shape[1]
    decode_steps = n_smp * (page_table.shape[1] // math.gcd(DECODE_PAGES_PER_STEP, page_table.shape[1]))
    if flash_steps == decode_steps:
        on, ons = _attention_fused(qt, kb, vt, qs, ks, vs, cache_kt, cache_v, page_table, lams,
                                   subln_col, subln)
    else:
        on = _flash_prompt(qt, kb, vt, lams, subln_col)
        ons = _decode_attn(qs, ks, vs, cache_kt, cache_v, page_table, lams, subln)
    k_p = kt.reshape(n_seq, 2 * DA_HEADS, DA_HD, seq).transpose(0, 3, 1, 2)
    v_p = v.reshape(n_seq, seq, DA_HEADS, 2 * DA_HD)
    xp = _proj_norm(on, xp, dw_out, g, b)
    k_s = ks.reshape(n_smp, 1, 2 * DA_HEADS, DA_HD)
    v_s = vs.reshape(n_smp, 1, DA_HEADS, 2 * DA_HD)
    xs = _proj_norm(ons.reshape(n_smp, d).astype(BF16), xs, dw_out, g, b)
    xp, xs = ffn_pair(1, 1, xp, xs)

    xp, xs = ffn_pair(2, 0, xp, xs)
    g, b = row(ln_g[2, 1]), row(ln_b[2, 1])
    hg_weights = (bf(hg_w_in), hg_lb_logits, row(hg_norm_g))
    dk = d // HG_HEADS
    xp, hg_p = _recurrent_prompt("hgrn", 2, xp, n_seq, HG_HEADS, dk, dk, hg_weights,
                                 bf(hg_w_out), g, b)
    xs, hg_s = _recurrent_sample("hgrn", 2, xs, state_hgrn, hg_weights, bf(hg_w_out), g, b)
    xp, xs = ffn_pair(2, 1, xp, xs)

    xp, xs = ffn_pair(3, 0, xp, xs)
    g, b = row(ln_g[3, 1]), row(ln_b[3, 1])
    main = 3 * d
    w_gk = jnp.pad(bf(gla_w_in[:, main:]), ((0, 0), (0, LANES - GLA_LOWRANK)))
    w_gk2 = jnp.pad(bf(gla_w_gk2), ((0, LANES - GLA_LOWRANK), (0, 0)))
    gla_weights = (bf(gla_w_in[:, :main]), w_gk, w_gk2, row(gla_b_gk2), row(gla_norm_g))
    gdk = d // 2 // GLA_HEADS
    gdv = d // GLA_HEADS
    xp, gla_p = _recurrent_prompt("gla", 3, xp, n_seq, GLA_HEADS, gdk, gdv, gla_weights,
                                  bf(gla_w_out), g, b)
    xs, gla_s = _recurrent_sample("gla", 3, xs, state_gla, gla_weights, bf(gla_w_out), g, b)
    xp, xs = ffn_pair(3, 1, xp, xs)

    return (xp.reshape(n_seq, seq, d), xs.reshape(n_smp, 1, d), conv_p, conv_s,
            k_p, v_p, k_s, v_s, hg_p, hg_s, gla_p, gla_s)
```

```python
import functools
import math

import jax
import jax.numpy as jnp
from jax import lax
from jax.experimental import pallas as pl
from jax.experimental.pallas import tpu as pltpu

F32 = jnp.float32
BF16 = jnp.bfloat16

DEPTH = 4
LN_EPS = 1e-5
RMS_EPS = 1e-6
ALPHA = (2.0 * DEPTH) ** 0.25
CONV_W = 3
DA_HEADS = 8
DA_HD = 64
ROT_DIM = DA_HD // 4
ROPE_THETA = 500000.0
DA_LAMBDA_INIT = 0.8 - 0.6 * math.exp(-0.3 * 1)
NEG_INF = -1e30
SCORE_SCALE = DA_HD ** -0.5 * math.log2(math.e)
PAGE_SIZE = 128
HG_HEADS = 8
GLA_HEADS = 4
GLA_LOWRANK = 16
GLA_GATE_NORM = 16.0

LANES = 128
SUBLANES = 8
MXU_DIM = 256
VMEM_LIMIT = 56 * 1024 * 1024

TOKEN_TILE = 512
FFN_TILE = 512
FFN_CHUNK = 1408
MIX_TILE = 256
REC_CHUNK = 64
ATT_TILE = 512
DECODE_PAGES_PER_STEP = 8


def _mm(a, b):
    return jnp.dot(a, b, preferred_element_type=F32)


def _mm_nt(a, b):
    return lax.dot_general(a, b, (((1,), (1,)), ((), ())), preferred_element_type=F32)


def _mm_tn(a, b):
    return lax.dot_general(a, b, (((0,), (0,)), ((), ())), preferred_element_type=F32)


def _layer_norm(z, g, b):
    mu = jnp.mean(z, axis=-1, keepdims=True)
    zc = z - mu
    var = jnp.mean(zc * zc, axis=-1, keepdims=True)
    return zc * lax.rsqrt(var + LN_EPS) * g + b


def _rms_norm(o, g):
    return o * lax.rsqrt(jnp.mean(o * o, axis=-1, keepdims=True) + RMS_EPS) * g


def _silu(a):
    return a * jax.nn.sigmoid(a)


def _params(n_axes):
    return pltpu.CompilerParams(dimension_semantics=("arbitrary",) * n_axes,
                                vmem_limit_bytes=VMEM_LIMIT)


def _const_spec(shape):
    nd = len(shape)
    return pl.BlockSpec(shape, lambda *_: (0,) * nd)


def _ffn_chunks(f):
    n = max(1, f // FFN_CHUNK)
    edges = [MXU_DIM * round(i * f / n / MXU_DIM) for i in range(n)] + [f]
    return list(zip(edges[:-1], edges[1:]))


def _ffn_body(nt, mixer_tail, x_ref, xs_ref, *refs):
    if mixer_tail:
        y_ref, ys_ref, wo_ref, g1_ref, b1_ref = refs[:5]
        refs = refs[5:]
    wu_ref, wd_ref, g_ref, b_ref, o_ref, os_ref, z_ref = refs
    i = pl.program_id(0)
    f = wd_ref.shape[0]
    g = g_ref[...]
    b = b_ref[...]

    def rows(r_ref, yr_ref):
        x = r_ref[...]
        if mixer_tail:
            x = _layer_norm(ALPHA * x + _mm(yr_ref[...], wo_ref[...]), g1_ref[...], b1_ref[...])
        return x

    def residual_plus_ffn(x, norm_of=None):
        xb = x.astype(BF16)
        acc = None
        normed = None
        for c0, c1 in _ffn_chunks(f):
            a = _mm(xb, wu_ref[:, c0:c1])
            u = _mm(xb, wu_ref[:, f + c0:f + c1])
            if norm_of is not None and normed is None:
                normed = _layer_norm(norm_of, g, b)
            h = (_silu(a) * u).astype(BF16)
            part = _mm(h, wd_ref[c0:c1, :])
            acc = part if acc is None else acc + part
        return ALPHA * x + 0.5 * acc, normed

    @pl.when(i == 0)
    def _init():
        z_ref[...] = jnp.zeros(z_ref.shape, F32)

    @pl.when(i < nt)
    def _tile():
        z_new, normed = residual_plus_ffn(rows(x_ref, y_ref if mixer_tail else None), z_ref[...])
        o_ref[...] = normed
        z_ref[...] = z_new

    @pl.when(i == nt)
    def _tail():
        o_ref[...] = _layer_norm(z_ref[...], g, b)
        zs, _ = residual_plus_ffn(rows(xs_ref, ys_ref if mixer_tail else None))
        os_ref[...] = _layer_norm(zs, g, b)


def _ffn(x, xs, w_up, w_down, layer, half, g, b, mixer_tail=None):
    t, d = x.shape
    ns = xs.shape[0]
    f = w_down.shape[2]
    tm = min(FFN_TILE, t)
    nt = t // tm
    once = pl.Buffered(1)
    tile = lambda i: (jnp.minimum(i, nt - 1), 0)
    tail_specs, tail_args = [], ()
    if mixer_tail is not None:
        tail_specs = [pl.BlockSpec((tm, d), tile), _const_spec((ns, d)),
                      pl.BlockSpec((d, d), lambda i: (0, 0), pipeline_mode=once),
                      _const_spec((1, d)), _const_spec((1, d))]
        tail_args = tuple(mixer_tail)
    return pl.pallas_call(
        functools.partial(_ffn_body, nt, mixer_tail is not None),
        grid=(nt + 1,),
        in_specs=[
            pl.BlockSpec((tm, d), tile),
            _const_spec((ns, d)),
        ] + tail_specs + [
            pl.BlockSpec((None, None, d, 2 * f), lambda i: (layer, half, 0, 0), pipeline_mode=once),
            pl.BlockSpec((None, None, f, d), lambda i: (layer, half, 0, 0), pipeline_mode=once),
            _const_spec((1, d)),
            _const_spec((1, d)),
        ],
        out_specs=[
            pl.BlockSpec((tm, d), lambda i: (jnp.maximum(i - 1, 0), 0)),
            _const_spec((ns, d)),
        ],
        out_shape=[jax.ShapeDtypeStruct((t, d), F32), jax.ShapeDtypeStruct((ns, d), F32)],
        scratch_shapes=[pltpu.VMEM((tm, d), F32)],
        compiler_params=_params(1),
        name="ffn",
    )(x, xs, *tail_args, w_up, w_down, g, b)


def _proj_norm_body(y_ref, x_ref, w_ref, g_ref, b_ref, o_ref):
    y = _mm(y_ref[...], w_ref[...])
    o_ref[...] = _layer_norm(ALPHA * x_ref[...] + y, g_ref[...], b_ref[...])


def _proj_norm(y, x, w, g, b):
    t, d = x.shape
    tm = min(TOKEN_TILE, t)
    return pl.pallas_call(
        _proj_norm_body,
        grid=(t // tm,),
        in_specs=[
            pl.BlockSpec((tm, d), lambda i: (i, 0)),
            pl.BlockSpec((tm, d), lambda i: (i, 0)),
            _const_spec((d, d)),
            _const_spec((1, d)),
            _const_spec((1, d)),
        ],
        out_specs=pl.BlockSpec((tm, d), lambda i: (i, 0)),
        out_shape=jax.ShapeDtypeStruct((t, d), F32),
        compiler_params=_params(1),
        name="proj_norm",
    )(y, x, w, g, b)


def _conv_prompt_body(x_ref, win_ref, cw_ref, wout_ref, g_ref, b_ref, o_ref, st_ref, ubuf_ref):
    t = pl.program_id(1)
    tm, d = x_ref.shape

    @pl.when(t == 0)
    def _reset():
        ubuf_ref[0:SUBLANES, :] = jnp.zeros((SUBLANES, d), F32)

    x = x_ref[...]
    p = _mm(x.astype(BF16), win_ref[...])
    u = p[:, d:2 * d] * p[:, 2 * d:]
    ubuf_ref[SUBLANES:, :] = u
    ue = ubuf_ref[...]
    um1 = pltpu.roll(ue, 1, 0)[SUBLANES:, :]
    um2 = pltpu.roll(ue, 2, 0)[SUBLANES:, :]
    cw = cw_ref[...]
    conv = cw[0:1, :] * um2 + cw[1:2, :] * um1 + cw[2:3, :] * u
    y = _mm((p[:, :d] * conv).astype(BF16), wout_ref[...])
    o_ref[...] = _layer_norm(ALPHA * x + y, g_ref[...], b_ref[...])
    ubuf_ref[0:SUBLANES, :] = ubuf_ref[tm:tm + SUBLANES, :]

    @pl.when(t == pl.num_programs(1) - 1)
    def _state():
        st_ref[0] = ubuf_ref[tm + SUBLANES - (CONV_W - 1):tm + SUBLANES, :]


def _conv_prompt(x, n_seq, w_in, conv_w, w_out, g, b):
    t, d = x.shape
    seq = t // n_seq
    tm = min(TOKEN_TILE, seq)
    nt = seq // tm
    return pl.pallas_call(
        _conv_prompt_body,
        grid=(n_seq, nt),
        in_specs=[
            pl.BlockSpec((tm, d), lambda s, i: (s * nt + i, 0)),
            _const_spec((d, 3 * d)),
            _const_spec((CONV_W, d)),
            _const_spec((d, d)),
            _const_spec((1, d)),
            _const_spec((1, d)),
        ],
        out_specs=[
            pl.BlockSpec((tm, d), lambda s, i: (s * nt + i, 0)),
            pl.BlockSpec((1, CONV_W - 1, d), lambda s, i: (s, 0, 0)),
        ],
        out_shape=[
            jax.ShapeDtypeStruct((t, d), F32),
            jax.ShapeDtypeStruct((n_seq, CONV_W - 1, d), F32),
        ],
        scratch_shapes=[pltpu.VMEM((tm + SUBLANES, d), F32)],
        compiler_params=_params(2),
        name="conv_prompt",
    )(x, w_in, conv_w, w_out, g, b)


def _conv_sample_body(x_ref, s0_ref, s1_ref, win_ref, cw_ref, wout_ref, g_ref, b_ref,
                      o_ref, u_ref):
    d = x_ref.shape[1]
    x = x_ref[...]
    p = _mm(x.astype(BF16), win_ref[...])
    u = p[:, d:2 * d] * p[:, 2 * d:]
    cw = cw_ref[...]
    conv = cw[0:1, :] * s0_ref[...] + cw[1:2, :] * s1_ref[...] + cw[2:3, :] * u
    y = _mm((p[:, :d] * conv).astype(BF16), wout_ref[...])
    o_ref[...] = _layer_norm(ALPHA * x + y, g_ref[...], b_ref[...])
    u_ref[...] = u


def _conv_sample(x, s0, s1, w_in, conv_w, w_out, g, b):
    n, d = x.shape
    return pl.pallas_call(
        _conv_sample_body,
        out_shape=[jax.ShapeDtypeStruct((n, d), F32), jax.ShapeDtypeStruct((n, d), F32)],
        compiler_params=pltpu.CompilerParams(vmem_limit_bytes=VMEM_LIMIT),
        name="conv_sample",
    )(x, s0, s1, w_in, conv_w, w_out, g, b)


def _rope_tables(pos):
    inv_freq = 1.0 / (ROPE_THETA ** (jnp.arange(0, ROT_DIM, 2, dtype=F32) / ROT_DIM))
    ang = pos.astype(F32)[:, None] * inv_freq[None, :]
    cos, sin = jnp.cos(ang), jnp.sin(ang)
    half = ROT_DIM // 2
    n = pos.shape[0]
    rest = DA_HD - ROT_DIM
    c64 = jnp.concatenate([cos, cos, jnp.ones((n, rest), F32)], axis=1)
    up64 = jnp.concatenate([-sin, jnp.zeros((n, DA_HD - half), F32)], axis=1)
    dn64 = jnp.concatenate([jnp.zeros((n, half), F32), sin, jnp.zeros((n, rest), F32)], axis=1)
    rep = LANES // DA_HD
    return (jnp.tile(c64, (1, rep)), jnp.tile(up64, (1, rep)), jnp.tile(dn64, (1, rep)),
            cos.T, sin.T)


def _rope_rows(x, c, s_up, s_dn):
    half = ROT_DIM // 2
    blocks = []
    for i in range(x.shape[1] // LANES):
        xb = x[:, i * LANES:(i + 1) * LANES]
        blocks.append(xb * c + pltpu.roll(xb, LANES - half, 1) * s_up + pltpu.roll(xb, half, 1) * s_dn)
    return jnp.concatenate(blocks, axis=1)


def _rope_cols(xt, ct, st):
    d, cols = xt.shape
    half = ROT_DIM // 2
    x3 = xt.reshape(d // DA_HD, DA_HD, cols)
    x1 = x3[:, 0:half, :]
    x2 = x3[:, half:ROT_DIM, :]
    x3 = jnp.concatenate([x1 * ct - x2 * st, x2 * ct + x1 * st, x3[:, ROT_DIM:, :]], axis=1)
    return x3.reshape(d, cols)


def _da_qkv_prompt_body(x_ref, wqt_ref, wkt_ref, wv_ref, ct_ref, st_ref,
                        qt_ref, kt_ref, kb_ref, v_ref, vt_ref):
    xb = x_ref[...].astype(BF16)
    ct = ct_ref[...]
    st = st_ref[...]
    qt = _rope_cols(_mm_nt(wqt_ref[...], xb), ct, st)
    qt_ref[0] = (qt * SCORE_SCALE).astype(BF16)
    kt = _rope_cols(_mm_nt(wkt_ref[...], xb), ct, st)
    kt_ref[0] = kt
    kb_ref[...] = kt.T.astype(BF16)
    v = _mm(xb, wv_ref[...])
    v_ref[...] = v
    vt_ref[0, 0] = v.T.astype(BF16)


def _da_qkv_prompt(x, n_seq, wqt, wkt, wv, tables):
    t, d = x.shape
    seq = t // n_seq
    tm = min(ATT_TILE, seq)
    nt = seq // tm
    ct, st = tables[3:]
    half = ROT_DIM // 2
    row = lambda s, i: (s * nt + i, 0)
    return pl.pallas_call(
        _da_qkv_prompt_body,
        grid=(n_seq, nt),
        in_specs=[
            pl.BlockSpec((tm, d), row),
            _const_spec((d, d)),
            _const_spec((d, d)),
            _const_spec((d, d)),
            pl.BlockSpec((half, tm), lambda s, i: (0, i)),
            pl.BlockSpec((half, tm), lambda s, i: (0, i)),
        ],
        out_specs=[
            pl.BlockSpec((1, d, tm), lambda s, i: (s, 0, i)),
            pl.BlockSpec((1, d, tm), lambda s, i: (s, 0, i)),
            pl.BlockSpec((tm, d), row),
            pl.BlockSpec((tm, d), row),
            pl.BlockSpec((1, 1, d, tm), lambda s, i: (s, i, 0, 0)),
        ],
        out_shape=[
            jax.ShapeDtypeStruct((n_seq, d, seq), BF16),
            jax.ShapeDtypeStruct((n_seq, d, seq), F32),
            jax.ShapeDtypeStruct((t, d), BF16),
            jax.ShapeDtypeStruct((t, d), F32),
            jax.ShapeDtypeStruct((n_seq, nt, d, tm), BF16),
        ],
        compiler_params=_params(2),
        name="da_qkv_prompt",
    )(x, wqt, wkt, wv, ct, st)


def _diff_lambda(lq1_ref, lk1_ref, lq2_ref, lk2_ref):
    s1 = jnp.sum(lq1_ref[...] * lk1_ref[...], axis=-1, keepdims=True)
    s2 = jnp.sum(lq2_ref[...] * lk2_ref[...], axis=-1, keepdims=True)
    return jnp.exp(s1) - jnp.exp(s2) + DA_LAMBDA_INIT


def _flash_body(lq1_ref, lk1_ref, lq2_ref, lk2_ref, *refs):
    _flash_step(pl.program_id(2), (lq1_ref, lk1_ref, lq2_ref, lk2_ref), *refs)


def _flash_step(qi, lam_refs, qt_ref, k_ref, vt_ref, sg_ref, o_ref, q2_ref, m_ref, l_ref, acc_ref):
    tq = qt_ref.shape[2]
    tk = vt_ref.shape[3]
    n_col = 2 * tq
    cb = min(2 * MXU_DIM, tq)

    qt = qt_ref[0]
    sub = lax.broadcasted_iota(jnp.int32, qt.shape, 0)
    zero = jnp.zeros_like(qt)
    q2_ref[:, 0:tq] = jnp.where(sub < DA_HD, qt, zero)
    q2_ref[:, tq:] = jnp.where(sub >= DA_HD, qt, zero)
    m_ref[...] = jnp.full(m_ref.shape, NEG_INF, F32)
    l_ref[...] = jnp.zeros(l_ref.shape, F32)
    acc_ref[...] = jnp.zeros(acc_ref.shape, F32)

    starts = list(range(0, n_col, cb))

    def kv_tiles(kjs, diagonal):
        k_tiles = [k_ref[pl.ds(pl.multiple_of(kj * tk, tk), tk), :] for kj in kjs]
        vt_tiles = [vt_ref[0, kj] for kj in kjs]
        items = [(t, c0) for t in range(len(kjs)) for c0 in starts]
        n_items = len(items)

        def score(n):
            t, c0 = items[n]
            return _mm(k_tiles[t], q2_ref[:, c0:c0 + cb])

        def max_pass(n, st):
            c0 = items[n][1]
            if diagonal:
                key = lax.broadcasted_iota(jnp.int32, st.shape, 0)
                qry = lax.broadcasted_iota(jnp.int32, st.shape, 1) + (c0 % tq)
                st = jnp.where(key <= qry, st, NEG_INF)
            m_prev = m_ref[:, c0:c0 + cb]
            return st, m_prev, jnp.maximum(m_prev, jnp.max(st, axis=0, keepdims=True))

        def exp_pass(n, st, m_prev, m_new):
            t, c0 = items[n]
            cs = slice(c0, c0 + cb)
            alpha = jnp.exp2(m_prev - m_new)
            pt = jnp.exp2(st - m_new)
            l_ref[:, cs] = alpha * l_ref[:, cs] + jnp.sum(pt, axis=0, keepdims=True)
            acc_ref[:, cs] = alpha * acc_ref[:, cs] + _mm(vt_tiles[t], pt.astype(BF16))
            m_ref[:, cs] = m_new

        raw = {n: score(n) for n in range(min(2, n_items))}
        ready = {0: max_pass(0, raw.pop(0))}
        for n in range(n_items):
            if n + 2 < n_items:
                raw[n + 2] = score(n + 2)
            if n + 1 < n_items:
                ready[n + 1] = max_pass(n + 1, raw.pop(n + 1))
            exp_pass(n, *ready.pop(n))

    def tile_pair(jj, carry):
        kv_tiles([2 * jj, 2 * jj + 1], False)
        return carry

    lax.fori_loop(0, lax.shift_right_logical(qi, 1), tile_pair, 0)

    @pl.when(jnp.bitwise_and(qi, 1) == 1)
    def _odd_tile():
        kv_tiles([qi - 1], False)

    kv_tiles([qi], True)

    lam = _diff_lambda(*lam_refs)
    o = acc_ref[...] / l_ref[...]
    o = o[:, 0:tq] - lam * o[:, tq:]
    r = lax.rsqrt(jnp.mean(o * o, axis=0, keepdims=True) + RMS_EPS)
    o = o * r * sg_ref[...] * (1.0 - DA_LAMBDA_INIT)
    o_ref[...] = o.T.astype(BF16)


def _flash_prompt(qt, kb, vt, lams, subln_col):
    n_seq, d, seq = qt.shape
    nq, ta = vt.shape[1], vt.shape[3]
    hw = 2 * DA_HD
    return pl.pallas_call(
        _flash_body,
        grid=(n_seq, DA_HEADS, nq),
        in_specs=[_const_spec((1, DA_HD))] * 4 + [
            pl.BlockSpec((1, hw, ta), lambda s, h, i: (s, h, i)),
            pl.BlockSpec((seq, hw), lambda s, h, i: (s, h)),
            pl.BlockSpec((1, nq, hw, ta), lambda s, h, i: (s, 0, h, 0)),
            _const_spec((hw, 1)),
        ],
        out_specs=pl.BlockSpec((ta, hw), lambda s, h, i: (s * nq + i, h)),
        out_shape=jax.ShapeDtypeStruct((n_seq * seq, d), BF16),
        scratch_shapes=[
            pltpu.VMEM((hw, 2 * ta), BF16),
            pltpu.VMEM((1, 2 * ta), F32),
            pltpu.VMEM((1, 2 * ta), F32),
            pltpu.VMEM((hw, 2 * ta), F32),
        ],
        compiler_params=_params(3),
        name="flash_prompt",
    )(*lams, qt, kb, vt, subln_col)


def _da_qkv_sample_body(x_ref, w_ref, c_ref, up_ref, dn_ref, q_ref, k_ref, v_ref):
    d = x_ref.shape[1]
    p = _mm(x_ref[...].astype(BF16), w_ref[...])
    c, up, dn = c_ref[...], up_ref[...], dn_ref[...]
    q_ref[...] = _rope_rows(p[:, :d], c, up, dn) * SCORE_SCALE
    k_ref[...] = _rope_rows(p[:, d:2 * d], c, up, dn)
    v_ref[...] = p[:, 2 * d:]


def _da_qkv_sample(x, w_in, tables):
    n, d = x.shape
    c, up, dn = tables[:3]
    return pl.pallas_call(
        _da_qkv_sample_body,
        out_shape=[jax.ShapeDtypeStruct((n, d), F32)] * 3,
        compiler_params=pltpu.CompilerParams(vmem_limit_bytes=VMEM_LIMIT),
        name="da_qkv_sample",
    )(x, w_in, c, up, dn)


def _lane_broadcast_columns(vec_bf, eye):
    w = vec_bf.shape[1]
    rows = jnp.broadcast_to(vec_bf, (LANES, w))
    return jnp.concatenate(
        [_mm_nt(eye, rows[:, i * LANES:(i + 1) * LANES]) for i in range(w // LANES)], axis=0)


def _identity_bf16():
    r = lax.broadcasted_iota(jnp.int32, (LANES, LANES), 0)
    c = lax.broadcasted_iota(jnp.int32, (LANES, LANES), 1)
    return jnp.where(r == c, 1.0, 0.0).astype(BF16)


def _decode_attn_body(npg, pt_ref, lq1_ref, lk1_ref, lq2_ref, lk2_ref, q_ref, kn_ref, vn_ref, *rest):
    del pt_ref
    _decode_step(pl.program_id(1), pl.num_programs(1) - 1, (lq1_ref, lk1_ref, lq2_ref, lk2_ref),
                 q_ref, kn_ref, vn_ref, rest[:npg], rest[npg:2 * npg], *rest[2 * npg:])


def _decode_step(j, j_last, lam_refs, q_ref, kn_ref, vn_ref, kt_refs, v_refs,
                 sg_ref, o_ref, qb_ref, ex_ref, m_ref, l_ref, acc_ref):
    npg = len(kt_refs)
    n_comp = 2 * DA_HEADS
    flat = PAGE_SIZE * DA_HEADS

    @pl.when(j == 0)
    def _init():
        qb = _lane_broadcast_columns(q_ref[0].astype(BF16), _identity_bf16())
        qb_ref[...] = qb.reshape(qb_ref.shape)
        key = lax.broadcasted_iota(jnp.int32, (PAGE_SIZE, flat), 0)
        col = lax.broadcasted_iota(jnp.int32, (PAGE_SIZE, flat), 1)
        ex_ref[...] = jnp.where(col // DA_HEADS == key, 1.0, 0.0).astype(BF16)
        m_ref[...] = jnp.full(m_ref.shape, NEG_INF, F32)
        l_ref[...] = jnp.zeros(l_ref.shape, F32)
        acc_ref[...] = jnp.zeros(acc_ref.shape, F32)

    def scores(kt_of_comp):
        parts = []
        for comp in range(2):
            qb = qb_ref[pl.ds(comp, DA_HEADS, stride=2)]
            parts.append(jnp.sum(qb * kt_of_comp(comp), axis=1))
        return jnp.concatenate(parts, axis=0)

    def online_update(s, pv_of_p):
        m_prev = m_ref[...]
        m_new = jnp.maximum(m_prev, jnp.max(s, axis=-1, keepdims=True))
        alpha = jnp.exp2(m_prev - m_new)
        p = jnp.exp2(s - m_new)
        l_ref[...] = alpha * l_ref[...] + jnp.sum(p, axis=-1, keepdims=True)
        acc_ref[...] = alpha * acc_ref[...] + pv_of_p(p)
        m_ref[...] = m_new

    def pages_pv(p):
        row = lax.broadcasted_iota(jnp.int32, (n_comp, flat), 0)
        col = lax.broadcasted_iota(jnp.int32, (n_comp, flat), 1)
        own_head = col % DA_HEADS == row % DA_HEADS
        pb = p.astype(BF16)
        ex = ex_ref[...]
        spread = [_mm(pb[:, g * PAGE_SIZE:(g + 1) * PAGE_SIZE], ex) for g in range(npg)]
        spread = [jnp.where(own_head, s, 0.0).astype(BF16) for s in spread]
        parts = [_mm(spread[g], v_refs[g][0].reshape(flat, LANES).astype(BF16))
                 for g in range(npg)]
        return functools.reduce(lambda a, b: a + b, parts)

    def page_scores():
        rows = []
        for r in range(n_comp):
            c = 2 * (r % DA_HEADS) + r // DA_HEADS
            qc = qb_ref[c]
            rows.append(jnp.concatenate(
                [jnp.sum(qc * kt[0, c], axis=0, keepdims=True) for kt in kt_refs], axis=1))
        return jnp.concatenate(rows, axis=0)

    online_update(page_scores(), pages_pv)

    @pl.when(j == j_last)
    def _finish():
        kb = _lane_broadcast_columns(kn_ref[0].astype(BF16), _identity_bf16())
        kb = kb.reshape(DA_HEADS, 2, DA_HD, LANES)
        s_self = scores(lambda comp: kb[:, comp])
        lane = lax.broadcasted_iota(jnp.int32, (n_comp, LANES), 1)
        s_self = jnp.where(lane == 0, s_self, NEG_INF)
        vn = vn_ref[0]
        v_rows = jnp.concatenate(
            [vn[:, h * LANES:(h + 1) * LANES] for h in range(DA_HEADS)] * 2, axis=0)
        online_update(s_self, lambda p: jnp.sum(p, axis=-1, keepdims=True) * v_rows)
        lam = _diff_lambda(*lam_refs)
        o = acc_ref[...] / l_ref[...]
        o = o[0:DA_HEADS, :] - lam * o[DA_HEADS:, :]
        o_ref[0] = _rms_norm(o, sg_ref[...]) * (1.0 - DA_LAMBDA_INIT)


def _decode_attn(q, k_new, v_new, cache_kt, cache_v, page_table, lams, subln_g):
    n, d = q.shape
    n_pages = page_table.shape[1]
    npg = math.gcd(DECODE_PAGES_PER_STEP, n_pages)
    n_comp = 2 * DA_HEADS
    hw = 2 * DA_HD
    row3 = lambda a: a.reshape(n, 1, d)
    vec_spec = pl.BlockSpec((1, 1, d), lambda s, j, pt: (s, 0, 0))
    small = lambda shape: pl.BlockSpec(shape, lambda s, j, pt: (0,) * len(shape))
    page = lambda g: (lambda s, j, pt: (pt[s * n_pages + j * npg + g], 0, 0, 0))
    grid_spec = pltpu.PrefetchScalarGridSpec(
        num_scalar_prefetch=1,
        grid=(n, n_pages // npg),
        in_specs=[small((1, DA_HD))] * 4 + [vec_spec, vec_spec, vec_spec]
        + [pl.BlockSpec((1, n_comp, DA_HD, PAGE_SIZE), page(g)) for g in range(npg)]
        + [pl.BlockSpec((1, PAGE_SIZE, DA_HEADS, hw), page(g)) for g in range(npg)]
        + [small((1, hw))],
        out_specs=pl.BlockSpec((1, DA_HEADS, hw), lambda s, j, pt: (s, 0, 0)),
        scratch_shapes=[
            pltpu.VMEM((n_comp, DA_HD, LANES), F32),
            pltpu.VMEM((PAGE_SIZE, PAGE_SIZE * DA_HEADS), BF16),
            pltpu.VMEM((n_comp, 1), F32),
            pltpu.VMEM((n_comp, 1), F32),
            pltpu.VMEM((n_comp, hw), F32),
        ],
    )
    return pl.pallas_call(
        functools.partial(_decode_attn_body, npg),
        grid_spec=grid_spec,
        out_shape=jax.ShapeDtypeStruct((n, DA_HEADS, hw), F32),
        compiler_params=_params(2),
        name="decode_attn",
    )(page_table.reshape(-1), *lams, row3(q), row3(k_new), row3(v_new),
      *([cache_kt] * npg), *([cache_v] * npg), subln_g)


def _attention_fused_body(nq, n_groups, npg, pt_ref, lq1_ref, lk1_ref, lq2_ref, lk2_ref,
                          qt_ref, k_ref, vt_ref, sgc_ref, q_ref, kn_ref, vn_ref, *rest):
    del pt_ref
    kt_refs, v_refs = rest[:npg], rest[npg:2 * npg]
    (sg_ref, of_ref, od_ref, q2_ref, fm_ref, fl_ref, facc_ref,
     qb_ref, ex_ref, dm_ref, dl_ref, dacc_ref) = rest[2 * npg:]
    step = pl.program_id(0)
    lam_refs = (lq1_ref, lk1_ref, lq2_ref, lk2_ref)
    _flash_step(lax.rem(step, nq), lam_refs, qt_ref, k_ref, vt_ref, sgc_ref, of_ref,
                q2_ref, fm_ref, fl_ref, facc_ref)
    _decode_step(lax.rem(step, n_groups), n_groups - 1, lam_refs, q_ref, kn_ref, vn_ref,
                 kt_refs, v_refs, sg_ref, od_ref, qb_ref, ex_ref, dm_ref, dl_ref, dacc_ref)


def _attention_fused(qt, kb, vt, q, k_new, v_new, cache_kt, cache_v, page_table, lams,
                     subln_col, subln_row):
    n_seq, d, seq = qt.shape
    nq, ta = vt.shape[1], vt.shape[3]
    n = q.shape[0]
    n_pages = page_table.shape[1]
    npg = math.gcd(DECODE_PAGES_PER_STEP, n_pages)
    n_groups = n_pages // npg
    steps = n_seq * DA_HEADS * nq
    assert steps == n * n_groups
    n_comp = 2 * DA_HEADS
    hw = 2 * DA_HD
    f_seq = lambda t: lax.div(t, DA_HEADS * nq)
    f_head = lambda t: lax.rem(lax.div(t, nq), DA_HEADS)
    f_tile = lambda t: lax.rem(t, nq)
    d_seq = lambda t: lax.div(t, n_groups)
    d_grp = lambda t: lax.rem(t, n_groups)
    row3 = lambda a: a.reshape(n, 1, d)
    small = lambda shape: pl.BlockSpec(shape, lambda t, pt: (0,) * len(shape))
    vec_spec = pl.BlockSpec((1, 1, d), lambda t, pt: (d_seq(t), 0, 0))
    page = lambda g: (lambda t, pt: (pt[d_seq(t) * n_pages + d_grp(t) * npg + g], 0, 0, 0))
    grid_spec = pltpu.PrefetchScalarGridSpec(
        num_scalar_prefetch=1,
        grid=(steps,),
        in_specs=[small((1, DA_HD))] * 4 + [
            pl.BlockSpec((1, hw, ta), lambda t, pt: (f_seq(t), f_head(t), f_tile(t))),
            pl.BlockSpec((seq, hw), lambda t, pt: (f_seq(t), f_head(t))),
            pl.BlockSpec((1, nq, hw, ta), lambda t, pt: (f_seq(t), 0, f_head(t), 0)),
            small((hw, 1)),
            vec_spec, vec_spec, vec_spec]
        + [pl.BlockSpec((1, n_comp, DA_HD, PAGE_SIZE), page(g)) for g in range(npg)]
        + [pl.BlockSpec((1, PAGE_SIZE, DA_HEADS, hw), page(g)) for g in range(npg)]
        + [small((1, hw))],
        out_specs=[
            pl.BlockSpec((ta, hw), lambda t, pt: (f_seq(t) * nq + f_tile(t), f_head(t))),
            pl.BlockSpec((1, DA_HEADS, hw), lambda t, pt: (d_seq(t), 0, 0)),
        ],
        scratch_shapes=[
            pltpu.VMEM((hw, 2 * ta), BF16),
            pltpu.VMEM((1, 2 * ta), F32),
            pltpu.VMEM((1, 2 * ta), F32),
            pltpu.VMEM((hw, 2 * ta), F32),
            pltpu.VMEM((n_comp, DA_HD, LANES), F32),
            pltpu.VMEM((PAGE_SIZE, PAGE_SIZE * DA_HEADS), BF16),
            pltpu.VMEM((n_comp, 1), F32),
            pltpu.VMEM((n_comp, 1), F32),
            pltpu.VMEM((n_comp, hw), F32),
        ],
    )
    return pl.pallas_call(
        functools.partial(_attention_fused_body, nq, n_groups, npg),
        grid_spec=grid_spec,
        out_shape=[jax.ShapeDtypeStruct((n_seq * seq, d), BF16),
                   jax.ShapeDtypeStruct((n, DA_HEADS, hw), F32)],
        compiler_params=_params(1),
        name="attention_fused",
    )(page_table.reshape(-1), *lams, qt, kb, vt, subln_col, row3(q), row3(k_new), row3(v_new),
      *([cache_kt] * npg), *([cache_v] * npg), subln_row)


def _cumsum_rows(g, tri):
    hi = g.astype(BF16)
    r1 = g - hi.astype(F32)
    mid = r1.astype(BF16)
    lo = (r1 - mid.astype(F32)).astype(BF16)
    return _mm(tri, hi) + _mm(tri, mid) + _mm(tri, lo)


def _recurrence_tile(q, k, v, g, st_ref, n_heads, dk, dv, c, post):
    t = q.shape[0]
    nc = t // c
    row = lax.broadcasted_iota(jnp.int32, (t, t), 0)
    col = lax.broadcasted_iota(jnp.int32, (t, t), 1)
    same_chunk = row // c == col // c
    tri = jnp.where(same_chunk & (row >= col), 1.0, 0.0).astype(BF16)
    all_g = _cumsum_rows(g, tri)
    q_state, q_dec, k_dec, k_tail, decay, vb = [], [], [], [], [], []
    for ci in range(nc):
        rows = slice(ci * c, (ci + 1) * c)
        big_g = all_g[rows]
        g_mid = all_g[ci * c + c // 2 - 1:ci * c + c // 2, :]
        g_last = all_g[(ci + 1) * c - 1:(ci + 1) * c, :]
        q_state.append((q[rows] * jnp.exp(big_g)).astype(BF16))
        q_dec.append((q[rows] * jnp.exp(big_g - g_mid)).astype(BF16))
        k_dec.append((k[rows] * jnp.exp(g_mid - big_g)).astype(BF16))
        k_tail.append((k[rows] * jnp.exp(g_last - big_g)).astype(BF16))
        decay.append(jnp.exp(g_last))
        vb.append(v[rows].astype(BF16))
    causal = (lax.broadcasted_iota(jnp.int32, (c, c), 0)
              >= lax.broadcasted_iota(jnp.int32, (c, c), 1))
    pairs = [(ci, h) for ci in range(nc) for h in range(n_heads)]
    ks = lambda h: slice(h * dk, (h + 1) * dk)
    vs = lambda h: slice(h * dv, (h + 1) * dv)
    scores = {p: _mm_nt(q_dec[p[0]][:, ks(p[1])], k_dec[p[0]][:, ks(p[1])]) for p in pairs}
    update = {p: _mm_tn(vb[p[0]][:, vs(p[1])], k_tail[p[0]][:, ks(p[1])]) for p in pairs}
    scores = {p: jnp.where(causal, s, 0.0).astype(BF16) for p, s in scores.items()}
    intra = {p: _mm(scores[p], vb[p[0]][:, vs(p[1])]) for p in pairs}
    entering = {}
    for h in range(n_heads):
        state = st_ref[h]
        for ci in range(nc):
            entering[ci, h] = state.astype(BF16)
            state = state * decay[ci][:, ks(h)] + update[ci, h]
        st_ref[h] = state
    inter = {p: _mm_nt(q_state[p[0]][:, ks(p[1])], entering[p]) for p in pairs}
    return jnp.concatenate(
        [jnp.concatenate([post(intra[ci, h] + inter[ci, h], h, ci) for h in range(n_heads)], axis=1)
         for ci in range(nc)], axis=0)


def _hgrn_lower_bound(lbl_ref, layer):
    logits = lbl_ref[...]
    e = jnp.exp(logits - jnp.max(logits, axis=0, keepdims=True))
    p = e / jnp.sum(e, axis=0, keepdims=True)
    return jnp.sum(p[0:layer + 1, :], axis=0, keepdims=True) - p[0:1, :]


def _hgrn_gates(p, lb, d):
    forget = lb + (1.0 - lb) * jax.nn.sigmoid(p[:, d:2 * d])
    return _silu(p[:, :d]), 1.0 - forget, p[:, 2 * d:], jnp.log(forget)


def _gla_gates(p, gk, wgk2_ref, bgk2_ref, d):
    key = d // 2
    z = _mm(gk.astype(BF16), wgk2_ref[...]) + bgk2_ref[...]
    log_sig = jnp.minimum(z, 0.0) - jnp.log(1.0 + jnp.exp(-jnp.abs(z)))
    q = p[:, :key] * ((key // GLA_HEADS) ** -0.5)
    return q, p[:, key:2 * key], p[:, 2 * key:2 * key + d], log_sig / GLA_GATE_NORM


def _recurrent_prompt_body(kind, layer, *refs):
    if kind == "hgrn":
        (x_ref, win_ref, lbl_ref, ng_ref, wout_ref, g_ref, b_ref,
         o_ref, sfin_ref, st_ref) = refs
    else:
        (x_ref, win_ref, wgk_ref, wgk2_ref, bgk2_ref, ng_ref, wout_ref, g_ref, b_ref,
         o_ref, sfin_ref, st_ref) = refs
    t = pl.program_id(1)
    tm, d = x_ref.shape
    n_heads, dv, dk = st_ref.shape

    @pl.when(t == 0)
    def _reset():
        st_ref[...] = jnp.zeros(st_ref.shape, F32)

    x = x_ref[...]
    xb = x.astype(BF16)
    p = _mm(xb, win_ref[...])
    ng = ng_ref[...]
    chunk = min(REC_CHUNK, tm)
    if kind == "hgrn":
        q, k, v, g = _hgrn_gates(p, _hgrn_lower_bound(lbl_ref, layer), d)
        post = lambda o, h, ci: _rms_norm(o, ng)
    else:
        q, k, v, g = _gla_gates(p, _mm(xb, wgk_ref[...]), wgk2_ref, bgk2_ref, d)
        gate = _silu(p[:, 2 * d:3 * d])
        post = lambda o, h, ci: (_rms_norm(o, ng)
                                 * gate[ci * chunk:(ci + 1) * chunk, h * dv:(h + 1) * dv])
    on = _recurrence_tile(q, k, v, g, st_ref, n_heads, dk, dv, chunk, post)
    y = _mm(on.astype(BF16), wout_ref[...])
    o_ref[...] = _layer_norm(ALPHA * x + y, g_ref[...], b_ref[...])

    @pl.when(t == pl.num_programs(1) - 1)
    def _state():
        for h in range(n_heads):
            sfin_ref[0, h] = st_ref[h].T


def _recurrent_prompt(kind, layer, x, n_seq, n_heads, dk, dv, weights, w_out, g, b):
    t, d = x.shape
    seq = t // n_seq
    tm = min(MIX_TILE, seq)
    nt = seq // tm
    row = lambda s, i: (s * nt + i, 0)
    w_specs = [_const_spec(w.shape) for w in weights]
    return pl.pallas_call(
        functools.partial(_recurrent_prompt_body, kind, layer),
        grid=(n_seq, nt),
        in_specs=[pl.BlockSpec((tm, d), row)] + w_specs + [
            _const_spec((d, d)), _const_spec((1, d)), _const_spec((1, d))],
        out_specs=[
            pl.BlockSpec((tm, d), row),
            pl.BlockSpec((1, n_heads, dk, dv), lambda s, i: (s, 0, 0, 0)),
        ],
        out_shape=[
            jax.ShapeDtypeStruct((t, d), F32),
            jax.ShapeDtypeStruct((n_seq, n_heads, dk, dv), F32),
        ],
        scratch_shapes=[pltpu.VMEM((n_heads, dv, dk), F32)],
        compiler_params=_params(2),
        name=kind + "_prompt",
    )(x, *weights, w_out, g, b)


def _recurrent_sample_body(kind, layer, *refs):
    if kind == "hgrn":
        (x_ref, s_ref, win_ref, lbl_ref, ng_ref, wout_ref, g_ref, b_ref,
         o_ref, snew_ref, q_scr, k_scr, v_scr, f_scr, gate_scr, on_scr) = refs
    else:
        (x_ref, s_ref, win_ref, wgk_ref, wgk2_ref, bgk2_ref, ng_ref, wout_ref, g_ref, b_ref,
         o_ref, snew_ref, q_scr, k_scr, v_scr, f_scr, gate_scr, on_scr) = refs
    n = pl.program_id(0)
    d = x_ref.shape[1]
    _, n_heads, dk, dv = s_ref.shape

    @pl.when(n == 0)
    def _project():
        xb = x_ref[...].astype(BF16)
        p = _mm(xb, win_ref[...])
        if kind == "hgrn":
            q, k, v, g = _hgrn_gates(p, _hgrn_lower_bound(lbl_ref, layer), d)
            gate_scr[...] = jnp.zeros(gate_scr.shape, F32)
        else:
            q, k, v, g = _gla_gates(p, _mm(xb, wgk_ref[...]), wgk2_ref, bgk2_ref, d)
            gate_scr[...] = p[:, 2 * d:3 * d]
        q_scr[...] = q
        k_scr[...] = k
        v_scr[...] = v
        f_scr[...] = jnp.exp(g)

    eye = _identity_bf16()
    q = q_scr[pl.ds(n, 1), :]
    k = k_scr[pl.ds(n, 1), :]
    v = v_scr[pl.ds(n, 1), :]
    f = f_scr[pl.ds(n, 1), :]
    f_hi = f.astype(BF16)
    f_r = f - f_hi.astype(F32)
    f_mid = f_r.astype(BF16)
    f_lo = (f_r - f_mid.astype(F32)).astype(BF16)
    f_col = (_lane_broadcast_columns(f_hi, eye) + _lane_broadcast_columns(f_mid, eye)
             + _lane_broadcast_columns(f_lo, eye))
    k_col = _lane_broadcast_columns(k.astype(BF16), eye)
    reps = dv // LANES
    widen = lambda a: a if reps == 1 else jnp.concatenate([a] * reps, axis=1)
    outs = []
    for h in range(n_heads):
        ks = slice(h * dk, (h + 1) * dk)
        vs = slice(h * dv, (h + 1) * dv)
        s_new = widen(f_col[ks]) * s_ref[0, h] + widen(k_col[ks]) * v[:, vs]
        snew_ref[0, h] = s_new
        q_rows = jnp.broadcast_to(q[:, ks], (2 * SUBLANES, dk)).astype(BF16)
        outs.append(_mm(q_rows, s_new.astype(BF16))[0:1, :])
    on_scr[pl.ds(n, 1), :] = jnp.concatenate(outs, axis=1)

    @pl.when(n == pl.num_programs(0) - 1)
    def _finish():
        ng = ng_ref[...]
        o = on_scr[...]
        parts = []
        for h in range(n_heads):
            oh = _rms_norm(o[:, h * dv:(h + 1) * dv], ng)
            if kind == "gla":
                oh = oh * _silu(gate_scr[:, h * dv:(h + 1) * dv])
            parts.append(oh)
        y = _mm(jnp.concatenate(parts, axis=1).astype(BF16), wout_ref[...])
        o_ref[...] = _layer_norm(ALPHA * x_ref[...] + y, g_ref[...], b_ref[...])


def _recurrent_sample(kind, layer, x, state, weights, w_out, g, b):
    n, d = x.shape
    _, n_heads, dk, dv = state.shape
    w_specs = [_const_spec(w.shape) for w in weights]
    st_spec = pl.BlockSpec((1, n_heads, dk, dv), lambda s: (s, 0, 0, 0))
    return pl.pallas_call(
        functools.partial(_recurrent_sample_body, kind, layer),
        grid=(n,),
        in_specs=[_const_spec((n, d)), st_spec] + w_specs + [
            _const_spec((d, d)), _const_spec((1, d)), _const_spec((1, d))],
        out_specs=[_const_spec((n, d)), st_spec],
        out_shape=[jax.ShapeDtypeStruct((n, d), F32), jax.ShapeDtypeStruct(state.shape, F32)],
        scratch_shapes=[
            pltpu.VMEM((n, n_heads * dk), F32),
            pltpu.VMEM((n, n_heads * dk), F32),
            pltpu.VMEM((n, n_heads * dv), F32),
            pltpu.VMEM((n, n_heads * dk), F32),
            pltpu.VMEM((n, d), F32),
            pltpu.VMEM((n, n_heads * dv), F32),
        ],
        compiler_params=_params(1),
        name=kind + "_sample",
    )(x, state, *weights, w_out, g, b)


def kernel(x_prompt, x_sample, state_conv, cache_k, cache_v, page_table, state_hgrn, state_gla,
           ffn_w_up, ffn_w_down, ln_g, ln_b,
           conv_w_in, conv_w, conv_w_out,
           da_w_in, da_lambda_q1, da_lambda_k1, da_lambda_q2, da_lambda_k2, da_subln_g, da_w_out,
           hg_w_in, hg_lb_logits, hg_norm_g, hg_w_out,
           gla_w_in, gla_w_gk2, gla_b_gk2, gla_norm_g, gla_w_out):
    n_seq, seq, d = x_prompt.shape
    n_smp = x_sample.shape[0]
    xp = x_prompt.reshape(n_seq * seq, d)
    xs = x_sample.reshape(n_smp, d)
    bf = lambda w: w.astype(BF16)
    row = lambda v: v.reshape(1, -1)
    w_up, w_down = bf(ffn_w_up), bf(ffn_w_down)

    def ffn_pair(i, half, xp, xs):
        g, b = row(ln_g[i, half * 2]), row(ln_b[i, half * 2])
        return _ffn(xp, xs, w_up, w_down, i, half, g, b)

    xp, xs = ffn_pair(0, 0, xp, xs)
    g, b = row(ln_g[0, 1]), row(ln_b[0, 1])
    cw_in, cw_out = bf(conv_w_in), bf(conv_w_out)
    xp, conv_p = _conv_prompt(xp, n_seq, cw_in, conv_w, cw_out, g, b)
    xs, u_s = _conv_sample(xs, state_conv[:, 0], state_conv[:, 1], cw_in, conv_w, cw_out, g, b)
    conv_s = jnp.stack([state_conv[:, 1], u_s], axis=1)
    xp, xs = ffn_pair(0, 1, xp, xs)

    xp, xs = ffn_pair(1, 0, xp, xs)
    g, b = row(ln_g[1, 1]), row(ln_b[1, 1])
    lams = [row(v) for v in (da_lambda_q1, da_lambda_k1, da_lambda_q2, da_lambda_k2)]
    subln = row(da_subln_g)
    dw_in, dw_out = bf(da_w_in), bf(da_w_out)
    past = page_table.shape[1] * PAGE_SIZE
    qt, kt, kb, v, vt = _da_qkv_prompt(
        xp, n_seq, dw_in[:, :d].T, dw_in[:, d:2 * d].T, dw_in[:, 2 * d:],
        _rope_tables(jnp.arange(seq, dtype=jnp.int32)))
    qs, ks, vs = _da_qkv_sample(xs, dw_in, _rope_tables(jnp.full((1,), past, jnp.int32)))
    subln_col = da_subln_g.reshape(-1, 1)
    cache_kt = cache_k.transpose(0, 2, 3, 1)
    flash_steps = n_seq * DA_HEADS * vt.shape[1]
    decode_steps = n_smp * (page_table.shape[1] // math.gcd(DECODE_PAGES_PER_STEP, page_table.shape[1]))
    if flash_steps == decode_steps:
        on, ons = _attention_fused(qt, kb, vt, qs, ks, vs, cache_kt, cache_v, page_table, lams,
                                   subln_col, subln)
    else:
        on = _flash_prompt(qt, kb, vt, lams, subln_col)
        ons = _decode_attn(qs, ks, vs, cache_kt, cache_v, page_table, lams, subln)
    k_p = kt.reshape(n_seq, 2 * DA_HEADS, DA_HD, seq).transpose(0, 3, 1, 2)
    v_p = v.reshape(n_seq, seq, DA_HEADS, 2 * DA_HD)
    k_s = ks.reshape(n_smp, 1, 2 * DA_HEADS, DA_HD)
    v_s = vs.reshape(n_smp, 1, DA_HEADS, 2 * DA_HD)
    xp, xs = _ffn(xp, xs, w_up, w_down, 1, 1, row(ln_g[1, 2]), row(ln_b[1, 2]),
                  mixer_tail=(on, ons.reshape(n_smp, d).astype(BF16), dw_out, g, b))

    xp, xs = ffn_pair(2, 0, xp, xs)
    g, b = row(ln_g[2, 1]), row(ln_b[2, 1])
    hg_weights = (bf(hg_w_in), hg_lb_logits, row(hg_norm_g))
    dk = d // HG_HEADS
    xp, hg_p = _recurrent_prompt("hgrn", 2, xp, n_seq, HG_HEADS, dk, dk, hg_weights,
                                 bf(hg_w_out), g, b)
    xs, hg_s = _recurrent_sample("hgrn", 2, xs, state_hgrn, hg_weights, bf(hg_w_out), g, b)
    xp, xs = ffn_pair(2, 1, xp, xs)

    xp, xs = ffn_pair(3, 0, xp, xs)
    g, b = row(ln_g[3, 1]), row(ln_b[3, 1])
    main = 3 * d
    w_gk = jnp.pad(bf(gla_w_in[:, main:]), ((0, 0), (0, LANES - GLA_LOWRANK)))
    w_gk2 = jnp.pad(bf(gla_w_gk2), ((0, LANES - GLA_LOWRANK), (0, 0)))
    gla_weights = (bf(gla_w_in[:, :main]), w_gk, w_gk2, row(gla_b_gk2), row(gla_norm_g))
    gdk = d // 2 // GLA_HEADS
    gdv = d // GLA_HEADS
    xp, gla_p = _recurrent_prompt("gla", 3, xp, n_seq, GLA_HEADS, gdk, gdv, gla_weights,
                                  bf(gla_w_out), g, b)
    xs, gla_s = _recurrent_sample("gla", 3, xs, state_gla, gla_weights, bf(gla_w_out), g, b)
    xp, xs = ffn_pair(3, 1, xp, xs)

    return (xp.reshape(n_seq, seq, d), xs.reshape(n_smp, 1, d), conv_p, conv_s,
            k_p, v_p, k_s, v_s, hg_p, hg_s, gla_p, gla_s)
```
